```python
import math
import numpy as np
import jax
import jax.numpy as jnp
from jax import lax

D_MODEL = 1024
BATCH = 8
SEQ = 8192
DEPTH = 2
DEC_BATCH = 8
DEC_SEQ = 64
PAST_LEN = 1024

CHUNK = 64
N_META = 16
PREFIX = 128
META_START = PREFIX - N_META
Q_BLOCK = 128
RMS_EPS = 1e-6
L2_EPS = 1e-6

SB_HEADS = 8
SB_DIM = 64
SB_W = SB_HEADS * SB_DIM
GDN_HEADS = 4
GDN_DK = 128
GDN_DV = 128
CONV_W = 4
GDN_QK = GDN_HEADS * GDN_DK
GDN_V = GDN_HEADS * GDN_DV
GDN_CONV_CH = 2 * GDN_QK + GDN_V
SW_HEADS = 16
SW_KV_HEADS = 2
SW_GROUP = SW_HEADS // SW_KV_HEADS
SW_DIM = 64
WINDOW = 128
WIN_CHUNKS = WINDOW // CHUNK
SW_BUF = WINDOW
ROPE_THETA = 10000.0
D_FF = 2816
N_EXPERTS = 8
TOP_K = 2
EXPERT_FF = 3584
MOE_BLOCK = 256

EVEN_SPLITS = (SB_W, SB_W, SB_W, GDN_CONV_CH, GDN_HEADS, GDN_HEADS, GDN_V)
EVEN_IN = sum(EVEN_SPLITS)
EVEN_MIX = SB_W + GDN_V
ODD_SPLITS = (SW_HEADS * SW_DIM, SW_KV_HEADS * SW_DIM, SW_KV_HEADS * SW_DIM)
ODD_IN = sum(ODD_SPLITS)
ODD_MIX = SW_HEADS * SW_DIM

kernel_name = 'hybrid_stream_sb_gdn_swa_moe'


def _split(x, sizes):
    return jnp.split(x, np.cumsum(sizes)[:-1].tolist(), axis=-1)


def rmsnorm(x, gain):
    x32 = x.astype(jnp.float32)
    y = x32 * lax.rsqrt(jnp.mean(x32 * x32, axis=-1, keepdims=True) + RMS_EPS)
    return (y * gain.astype(jnp.float32)).astype(x.dtype)


def l2norm(x):
    x32 = x.astype(jnp.float32)
    return x32 * lax.rsqrt(jnp.sum(x32 * x32, axis=-1, keepdims=True) + L2_EPS)


def rope(x, pos):
    half = x.shape[-1] // 2
    inv_freq = ROPE_THETA ** (-jnp.arange(half, dtype=jnp.float32) / half)
    ang = pos.astype(jnp.float32)[:, None] * inv_freq[None, :]
    cos = jnp.cos(ang)[None, :, None, :]
    sin = jnp.sin(ang)[None, :, None, :]
    x32 = x.astype(jnp.float32)
    x1, x2 = x32[..., :half], x32[..., half:]
    return jnp.concatenate([x1 * cos - x2 * sin, x2 * cos + x1 * sin], axis=-1).astype(x.dtype)


def swiglu(u, w1, w3, w2):
    return (jax.nn.silu(u @ w1) * (u @ w3)) @ w2


def stick_breaking(q, k, v, q_pos, k_valid):
    z = jnp.einsum('bhqd,bhkd->bhqk', q, k).astype(jnp.float32) * (q.shape[-1] ** -0.5)
    visible = (jnp.arange(k.shape[2])[None, :] < q_pos[:, None]) & k_valid[None, :]
    log_keep = jnp.where(visible, jax.nn.log_sigmoid(-z), 0.0)
    later = lax.cumsum(log_keep, axis=3, reverse=True) - log_keep
    w = jnp.where(visible, jnp.exp(jax.nn.log_sigmoid(z) + later), 0.0)
    return jnp.einsum('bhqk,bhkd->bhqd', w, v.astype(jnp.float32)).astype(v.dtype)


def sb_prompt(q, k, v):
    nb, L, H, d = q.shape
    n_blk = L // Q_BLOCK
    kt = k.transpose(0, 2, 1, 3)
    vt = v.transpose(0, 2, 1, 3)
    k_valid = jnp.arange(L) >= META_START
    qb = q.transpose(0, 2, 1, 3).reshape(nb, H, n_blk, Q_BLOCK, d).transpose(2, 0, 1, 3, 4)
    starts = jnp.arange(n_blk, dtype=jnp.int32) * Q_BLOCK

    def block(args):
        qi, s0 = args
        return stick_breaking(qi, kt, vt, s0 + jnp.arange(Q_BLOCK, dtype=jnp.int32), k_valid)

    o = lax.map(block, (qb, starts))
    return o.transpose(1, 0, 3, 2, 4).reshape(nb, L, H * d)


def sb_sample(q, k, v, cache_k, cache_v):
    nb, T, H, d = q.shape
    past = cache_k.shape[1]
    kk = jnp.concatenate([cache_k, k], axis=1).transpose(0, 2, 1, 3)
    vv = jnp.concatenate([cache_v, v], axis=1).transpose(0, 2, 1, 3)
    q_pos = past + jnp.arange(T, dtype=jnp.int32)
    o = stick_breaking(q.transpose(0, 2, 1, 3), kk, vv, q_pos, jnp.ones((past + T,), bool))
    return o.transpose(0, 2, 1, 3).reshape(nb, T, H * d)


def gated_delta_chunked(q, k, v, g, beta, s0, chunk):
    nb, L, H, DK = q.shape
    DV = v.shape[-1]
    n = L // chunk

    def blocks(x):
        x = x.astype(jnp.float32).reshape((nb, n, chunk) + x.shape[2:])
        return jnp.moveaxis(jnp.moveaxis(x, 1, 0), 2, 3)

    qc = blocks(q) * (DK ** -0.5)
    kc, vc, gc, bc = blocks(k), blocks(v), blocks(g), blocks(beta)
    G = jnp.cumsum(gc, axis=-1)
    incl = jnp.tril(jnp.ones((chunk, chunk), dtype=bool))
    strict = jnp.tril(jnp.ones((chunk, chunk), dtype=bool), k=-1)
    decay = jnp.exp(jnp.where(incl, G[..., :, None] - G[..., None, :], -jnp.inf))
    kk = jnp.einsum('nbhcd,nbhed->nbhce', kc, kc)
    m = jnp.where(strict, bc[..., :, None] * kk * decay, 0.0)
    a = m + jnp.eye(chunk, dtype=jnp.float32)
    rhs = jnp.concatenate([bc[..., None] * vc, (bc * jnp.exp(G))[..., None] * kc], axis=-1)
    sol = lax.linalg.triangular_solve(a, rhs, left_side=True, lower=True, unit_diagonal=True)
    u_v, w_k = sol[..., :DV], sol[..., DV:]
    qk = jnp.einsum('nbhcd,nbhed->nbhce', qc, kc) * decay
    q_dec = qc * jnp.exp(G)[..., None]
    k_tail = kc * jnp.exp(G[..., -1:] - G)[..., None]
    g_end = jnp.exp(G[..., -1])

    def step(s, xs):
        u_v_i, w_i, qk_i, qd_i, kt_i, ge_i = xs
        u = u_v_i - jnp.einsum('bhck,bhkv->bhcv', w_i, s)
        o = jnp.einsum('bhck,bhkv->bhcv', qd_i, s) + jnp.einsum('bhce,bhev->bhcv', qk_i, u)
        s = ge_i[..., None, None] * s + jnp.einsum('bhck,bhcv->bhkv', kt_i, u)
        return s, o

    s_end, o = lax.scan(step, s0.astype(jnp.float32), (u_v, w_k, qk, q_dec, k_tail, g_end))
    o = jnp.moveaxis(jnp.moveaxis(o, 3, 2), 0, 1).reshape(nb, L, H, DV)
    return o, s_end


def gdn_mixer(xb, b_raw, a_raw, z, conv_buf, s0, valid, chunk, conv_w, a_log, dt_bias, gnorm):
    nb, L, _ = xb.shape
    xp = jnp.concatenate([conv_buf.astype(xb.dtype), xb], axis=1)
    c = lax.conv_general_dilated(xp, conv_w[:, None, :].astype(xb.dtype), (1,), 'VALID',
                                 dimension_numbers=('NWC', 'WIO', 'NWC'),
                                 feature_group_count=GDN_CONV_CH)
    c = jax.nn.silu(c)
    q, k, v = _split(c, (GDN_QK, GDN_QK, GDN_V))
    q = l2norm(q.reshape(nb, L, GDN_HEADS, GDN_DK))
    k = l2norm(k.reshape(nb, L, GDN_HEADS, GDN_DK))
    v = v.reshape(nb, L, GDN_HEADS, GDN_DV).astype(jnp.float32)
    msk = valid[None, :, None]
    beta = jnp.where(msk, jax.nn.sigmoid(b_raw.astype(jnp.float32)), 0.0)
    g = jnp.where(msk, -jnp.exp(a_log.astype(jnp.float32))
                  * jax.nn.softplus(a_raw.astype(jnp.float32) + dt_bias.astype(jnp.float32)), 0.0)
    o, s_new = gated_delta_chunked(q, k, v, g, beta, s0, chunk)
    o = o * lax.rsqrt(jnp.mean(o * o, axis=-1, keepdims=True) + RMS_EPS) * gnorm.astype(jnp.float32)
    o = o * jax.nn.silu(z.astype(jnp.float32)).reshape(nb, L, GDN_HEADS, GDN_DV)
    return o.reshape(nb, L, GDN_V).astype(xb.dtype), s_new.astype(s0.dtype), xp[:, -(CONV_W - 1):]


def sink_softmax(s, sinks):
    snk = jnp.broadcast_to(sinks.astype(jnp.float32).reshape(SW_KV_HEADS, SW_GROUP, 1, 1),
                           s.shape[:-1] + (1,))
    return jax.nn.softmax(jnp.concatenate([s, snk], axis=-1), axis=-1)[..., :-1]


def sw_prompt(q, k, v, sinks):
    nb, L = q.shape[:2]
    nc = L // CHUNK
    qc = q.reshape(nb, nc, CHUNK, SW_KV_HEADS, SW_GROUP, SW_DIM)

    def band(x):
        xc = x.reshape(nb, nc, CHUNK, SW_KV_HEADS, SW_DIM)
        xp = jnp.pad(xc, ((0, 0), (WIN_CHUNKS, 0), (0, 0), (0, 0), (0, 0)))
        return jnp.concatenate([xp[:, i:i + nc] for i in range(WIN_CHUNKS + 1)], axis=2)

    kb, vb = band(k), band(v)
    key_pos = ((jnp.arange(nc)[:, None] - WIN_CHUNKS) * CHUNK
               + jnp.arange((WIN_CHUNKS + 1) * CHUNK)[None, :])
    key_ok = key_pos >= META_START
    s = jnp.einsum('bnqkgd,bnskd->bnkgqs', qc, kb).astype(jnp.float32) * (SW_DIM ** -0.5)
    s = jnp.where(key_ok[None, :, None, None, None, :], s, -jnp.inf)
    p = sink_softmax(s, sinks)
    o = jnp.einsum('bnkgqs,bnskd->bnqkgd', p, vb.astype(jnp.float32))
    return o.reshape(nb, L, SW_HEADS * SW_DIM).astype(q.dtype)


def sw_sample(q, k, v, cache_k, cache_v, sinks):
    nb, T = q.shape[:2]
    kk = jnp.concatenate([cache_k, k], axis=1)
    vv = jnp.concatenate([cache_v, v], axis=1)
    qg = q.reshape(nb, T, SW_KV_HEADS, SW_GROUP, SW_DIM)
    s = jnp.einsum('bqkgd,bskd->bkgqs', qg, kk).astype(jnp.float32) * (SW_DIM ** -0.5)
    p = sink_softmax(s, sinks)
    o = jnp.einsum('bkgqs,bskd->bqkgd', p, vv.astype(jnp.float32))
    return o.reshape(nb, T, SW_HEADS * SW_DIM).astype(q.dtype)


def moe_swiglu(u, router_w, router_b, w1, w3, w2):
    shape = u.shape
    xs = u.reshape(-1, shape[-1])
    n_tok = xs.shape[0]
    n_asg = n_tok * TOP_K
    logits = (xs @ router_w).astype(jnp.float32) + router_b.astype(jnp.float32)
    top_val, top_idx = lax.top_k(logits, TOP_K)
    gates = jax.nn.softmax(top_val, axis=-1).reshape(-1)
    expert = top_idx.reshape(-1)
    token = jnp.repeat(jnp.arange(n_tok, dtype=jnp.int32), TOP_K)
    order = jnp.argsort(expert)
    e_s, t_s, g_s = expert[order], token[order], gates[order]
    counts = jnp.bincount(expert, length=N_EXPERTS)
    start = jnp.cumsum(counts) - counts
    padded = (counts + MOE_BLOCK - 1) // MOE_BLOCK * MOE_BLOCK
    pend = jnp.cumsum(padded)
    pstart = pend - padded
    dest = pstart[e_s] + jnp.arange(n_asg) - start[e_s]
    n_blk = (n_asg + N_EXPERTS * (MOE_BLOCK - 1) + MOE_BLOCK - 1) // MOE_BLOCK
    n_slot = n_blk * MOE_BLOCK
    slot_tok = jnp.full((n_slot,), n_tok, jnp.int32).at[dest].set(t_s)
    slot_gate = jnp.zeros((n_slot,), jnp.float32).at[dest].set(g_s)
    blk_expert = jnp.minimum(jnp.searchsorted(pend, jnp.arange(n_blk) * MOE_BLOCK, side='right'),
                             N_EXPERTS - 1)
    xs_pad = jnp.concatenate([xs, jnp.zeros((1, shape[-1]), xs.dtype)], axis=0)

    def expert_block(args):
        toks, gts, e = args
        xb = xs_pad[toks]
        hb = jax.nn.silu(xb @ w1[e]) * (xb @ w3[e])
        return (hb @ w2[e]).astype(jnp.float32) * gts[:, None]

    yb = lax.map(expert_block, (slot_tok.reshape(n_blk, MOE_BLOCK),
                                slot_gate.reshape(n_blk, MOE_BLOCK), blk_expert))
    y = jnp.zeros((n_tok + 1, shape[-1]), jnp.float32).at[slot_tok].add(yb.reshape(n_slot, shape[-1]))
    return y[:n_tok].reshape(shape).astype(u.dtype)


def even_layer(h, valid, cache_k, cache_v, conv_buf, s0, chunk, g_mix, g_ffn,
               w_in, conv_w, a_log, dt_bias, gnorm, w_out, w1, w3, w2):
    nb, L, _ = h.shape
    u = rmsnorm(h, g_mix)
    qa, ka, va, xb, b_raw, a_raw, z = _split(u @ w_in, EVEN_SPLITS)
    qa = qa.reshape(nb, L, SB_HEADS, SB_DIM)
    ka = ka.reshape(nb, L, SB_HEADS, SB_DIM)
    va = va.reshape(nb, L, SB_HEADS, SB_DIM)
    if cache_k is None:
        oa = sb_prompt(qa, ka, va)
    else:
        oa = sb_sample(qa, ka, va, cache_k, cache_v)
    ob, s_new, conv_new = gdn_mixer(xb, b_raw, a_raw, z, conv_buf, s0, valid, chunk,
                                    conv_w, a_log, dt_bias, gnorm)
    h = h + jnp.concatenate([oa, ob], axis=-1) @ w_out
    h = h + swiglu(rmsnorm(h, g_ffn), w1, w3, w2)
    return h, ka, va, s_new, conv_new


def odd_layer(h, pos, cache_k, cache_v, g_mix, g_ffn, w_in, sinks, w_out,
              router_w, router_b, w1, w3, w2):
    nb, L, _ = h.shape
    u = rmsnorm(h, g_mix)
    q, k, v = _split(u @ w_in, ODD_SPLITS)
    q = rope(q.reshape(nb, L, SW_HEADS, SW_DIM), pos)
    k = rope(k.reshape(nb, L, SW_KV_HEADS, SW_DIM), pos)
    v = v.reshape(nb, L, SW_KV_HEADS, SW_DIM)
    if cache_k is None:
        o = sw_prompt(q, k, v, sinks)
    else:
        o = sw_sample(q, k, v, cache_k, cache_v, sinks)
    h = h + o @ w_out
    h = h + moe_swiglu(rmsnorm(h, g_ffn), router_w, router_b, w1, w3, w2)
    return h, k, v


def setup_inputs(seed: int = 0) -> dict:
    key = jax.random.key(seed)
    ks = jax.random.split(key, 32)
    f32 = jnp.float32

    def nrm(k, shape, scale=1.0):
        return jax.random.normal(k, shape, f32) * scale

    sw_buf = min(WINDOW, PAST_LEN)
    a_init = jax.random.uniform(ks[13], (GDN_HEADS,), f32, 1.0, 16.0)
    dt = jnp.exp(jax.random.uniform(ks[14], (GDN_HEADS,), f32, math.log(1e-3), math.log(1e-1)))
    return {
        'x_prompt': nrm(ks[0], (BATCH, SEQ, D_MODEL)),
        'x_sample': nrm(ks[1], (DEC_BATCH, DEC_SEQ, D_MODEL)),
        'cache_sb_k': nrm(ks[2], (DEC_BATCH, PAST_LEN, SB_HEADS, SB_DIM)),
        'cache_sb_v': nrm(ks[3], (DEC_BATCH, PAST_LEN, SB_HEADS, SB_DIM)),
        'state_gdn': nrm(ks[4], (DEC_BATCH, GDN_HEADS, GDN_DK, GDN_DV), 0.1),
        'state_gdn_conv': nrm(ks[5], (DEC_BATCH, CONV_W - 1, GDN_CONV_CH)),
        'cache_sw_k': nrm(ks[6], (DEC_BATCH, sw_buf, SW_KV_HEADS, SW_DIM)),
        'cache_sw_v': nrm(ks[7], (DEC_BATCH, sw_buf, SW_KV_HEADS, SW_DIM)),
        'meta_tokens': nrm(ks[8], (N_META, D_MODEL)),
        'norm_mix': 1.0 + nrm(ks[9], (DEPTH, D_MODEL), 0.01),
        'norm_ffn': 1.0 + nrm(ks[10], (DEPTH, D_MODEL), 0.01),
        'norm_final': 1.0 + nrm(ks[11], (D_MODEL,), 0.01),
        'w_in_even': nrm(ks[12], (D_MODEL, EVEN_IN), D_MODEL ** -0.5),
        'gdn_conv_w': nrm(ks[15], (CONV_W, GDN_CONV_CH), CONV_W ** -0.5),
        'gdn_a_log': jnp.log(a_init),
        'gdn_dt_bias': dt + jnp.log(-jnp.expm1(-dt)),
        'gdn_norm': 1.0 + nrm(ks[16], (GDN_DV,), 0.01),
        'w_out_even': nrm(ks[17], (EVEN_MIX, D_MODEL), EVEN_MIX ** -0.5),
        'ffn_w1': nrm(ks[18], (D_MODEL, D_FF), D_MODEL ** -0.5),
        'ffn_w3': nrm(ks[19], (D_MODEL, D_FF), D_MODEL ** -0.5),
        'ffn_w2': nrm(ks[20], (D_FF, D_MODEL), D_FF ** -0.5),
        'w_in_odd': nrm(ks[21], (D_MODEL, ODD_IN), D_MODEL ** -0.5),
        'sw_sinks': nrm(ks[22], (SW_HEADS,), 1.0),
        'w_out_odd': nrm(ks[23], (ODD_MIX, D_MODEL), ODD_MIX ** -0.5),
        'router_w': nrm(ks[24], (D_MODEL, N_EXPERTS), D_MODEL ** -0.5),
        'router_b': nrm(ks[25], (N_EXPERTS,), 0.01),
        'moe_w1': nrm(ks[26], (N_EXPERTS, D_MODEL, EXPERT_FF), D_MODEL ** -0.5),
        'moe_w3': nrm(ks[27], (N_EXPERTS, D_MODEL, EXPERT_FF), D_MODEL ** -0.5),
        'moe_w2': nrm(ks[28], (N_EXPERTS, EXPERT_FF, D_MODEL), EXPERT_FF ** -0.5),
    }


def reference(x_prompt, x_sample, cache_sb_k, cache_sb_v, state_gdn, state_gdn_conv,
              cache_sw_k, cache_sw_v, meta_tokens, norm_mix, norm_ffn, norm_final,
              w_in_even, gdn_conv_w, gdn_a_log, gdn_dt_bias, gdn_norm, w_out_even,
              ffn_w1, ffn_w3, ffn_w2, w_in_odd, sw_sinks, w_out_odd,
              router_w, router_b, moe_w1, moe_w3, moe_w2):
    dt = x_prompt.dtype
    nb, seq = x_prompt.shape[:2]
    tn = x_sample.shape[1]
    past = cache_sb_k.shape[1]
    lp = PREFIX + seq
    hp = jnp.concatenate([jnp.zeros((nb, META_START, D_MODEL), dt),
                          jnp.broadcast_to(meta_tokens.astype(dt), (nb, N_META, D_MODEL)),
                          x_prompt], axis=1)
    valid_p = jnp.arange(lp) >= META_START
    pos_p = jnp.arange(lp, dtype=jnp.int32) - META_START
    hs = x_sample
    valid_s = jnp.ones((tn,), bool)
    pos_s = N_META + past + jnp.arange(tn, dtype=jnp.int32)

    for layer in range(DEPTH):
        g_mix, g_ffn = norm_mix[layer], norm_ffn[layer]
        if layer % 2 == 0:
            hp, sbk_p, sbv_p, gdn_p, conv_p = even_layer(
                hp, valid_p, None, None,
                jnp.zeros((nb, CONV_W - 1, GDN_CONV_CH), dt),
                jnp.zeros((nb, GDN_HEADS, GDN_DK, GDN_DV), dt), CHUNK, g_mix, g_ffn,
                w_in_even, gdn_conv_w, gdn_a_log, gdn_dt_bias, gdn_norm, w_out_even,
                ffn_w1, ffn_w3, ffn_w2)
            hs, sbk_s, sbv_s, gdn_s, conv_s = even_layer(
                hs, valid_s, cache_sb_k, cache_sb_v, state_gdn_conv, state_gdn, tn, g_mix, g_ffn,
                w_in_even, gdn_conv_w, gdn_a_log, gdn_dt_bias, gdn_norm, w_out_even,
                ffn_w1, ffn_w3, ffn_w2)
        else:
            hp, swk_p, swv_p = odd_layer(hp, pos_p, None, None, g_mix, g_ffn, w_in_odd, sw_sinks,
                                         w_out_odd, router_w, router_b, moe_w1, moe_w3, moe_w2)
            hs, swk_s, swv_s = odd_layer(hs, pos_s, cache_sw_k, cache_sw_v, g_mix, g_ffn, w_in_odd,
                                         sw_sinks, w_out_odd, router_w, router_b,
                                         moe_w1, moe_w3, moe_w2)
        hp = jnp.where(valid_p[None, :, None], hp, jnp.zeros((), dt))

    y_prompt = rmsnorm(hp, norm_final)[:, PREFIX:]
    y_sample = rmsnorm(hs, norm_final)
    buf = cache_sw_k.shape[1]
    sw_k_s = jnp.concatenate([cache_sw_k, swk_s], axis=1)[:, -buf:]
    sw_v_s = jnp.concatenate([cache_sw_v, swv_s], axis=1)[:, -buf:]
    return (y_prompt, y_sample,
            sbk_p[:, META_START:], sbv_p[:, META_START:], sbk_s, sbv_s,
            gdn_p, gdn_s, conv_p, conv_s,
            swk_p[:, -SW_BUF:], swv_p[:, -SW_BUF:], sw_k_s, sw_v_s)
```

```python
import functools
import math

import numpy as np
import jax
import jax.numpy as jnp
from jax import lax
from jax.experimental import pallas as pl
from jax.experimental.pallas import tpu as pltpu

F32 = jnp.float32
BF16 = jnp.bfloat16
HI = lax.Precision.HIGHEST

D_MODEL = 1024
CHUNK = 64
N_META_TOK = 16
PREFIX = 128
FIRST_VALID = PREFIX - N_META_TOK
RMS_EPS = 1e-6
L2_EPS = 1e-6

SB_HEADS = 8
SB_DIM = 64
SB_W = SB_HEADS * SB_DIM
GDN_HEADS = 4
GDN_DK = 128
GDN_DV = 128
CONV_W = 4
GDN_QK = GDN_HEADS * GDN_DK
GDN_V = GDN_HEADS * GDN_DV
GDN_CONV_CH = 2 * GDN_QK + GDN_V
GDN_IN_W = GDN_CONV_CH + GDN_V + 128
SW_HEADS = 16
SW_KV_HEADS = 2
SW_GROUP = SW_HEADS // SW_KV_HEADS
SW_DIM = 64
WINDOW = 128
WIN_CHUNKS = WINDOW // CHUNK
ROPE_THETA = 10000.0
D_FF = 2816
N_EXPERTS = 8
TOP_K = 2
EXPERT_FF = 3584

LANES = 128
VMEM_LIMIT = 56 * 1024 * 1024
NEG_BIG = -1e30
SB_EXIT = 120.0


def _cparams(sem):
    return pltpu.CompilerParams(dimension_semantics=sem, vmem_limit_bytes=VMEM_LIMIT)


def _const_spec(shape):
    nd = len(shape)
    return pl.BlockSpec(shape, lambda *_: (0,) * nd, pipeline_mode=pl.Buffered(1))


def _rms(x, g):
    ms = jnp.mean(x * x, axis=-1, keepdims=True)
    return x * lax.rsqrt(ms + RMS_EPS) * g


def _col_chunks(c0, c1, step=512):
    out = []
    while c0 < c1:
        out.append((c0, min(c0 + step, c1)))
        c0 = out[-1][1]
    return out


def _norm_proj_kernel(*refs, groups, rope_cols):
    if rope_cols:
        x_ref, g_ref, w_ref, cos_ref, sin_ref = refs[:5]
        outs = refs[5:]
    else:
        x_ref, g_ref, w_ref = refs[:3]
        outs = refs[3:]
    u = _rms(x_ref[...], g_ref[...]).astype(BF16)
    if rope_cols:
        cos = cos_ref[...]
        sin = sin_ref[...]
        lane = lax.broadcasted_iota(jnp.int32, cos.shape, 1)
        first_half = (lane % SW_DIM) < (SW_DIM // 2)
    oi = 0
    for (c0, c1, dtypes) in groups:
        for (a, b) in _col_chunks(c0, c1):
            y = jnp.dot(u, w_ref[:, a:b], preferred_element_type=F32)
            if a < rope_cols:
                pieces = []
                for p in range(0, b - a, LANES):
                    yp = y[:, p:p + LANES]
                    if a + p < rope_cols:
                        rot = jnp.where(first_half, pltpu.roll(yp, LANES - SW_DIM // 2, 1),
                                        pltpu.roll(yp, SW_DIM // 2, 1))
                        yp = yp * cos + rot * sin
                    pieces.append(yp)
                y = jnp.concatenate(pieces, axis=1) if len(pieces) > 1 else pieces[0]
            for k, dt in enumerate(dtypes):
                outs[oi + k][:, a - c0:b - c0] = y.astype(dt)
        oi += len(dtypes)


def norm_proj(x, g, w, groups, tm, rope=None):
    M, D = x.shape
    N = w.shape[1]
    assert M % tm == 0
    rope_cols = rope[2] if rope is not None else 0
    in_specs = [pl.BlockSpec((tm, D), lambda i: (i, 0)), _const_spec((1, D)), _const_spec((D, N))]
    args = [x, g, w]
    if rope is not None:
        in_specs += [pl.BlockSpec((tm, LANES), lambda i: (i, 0))] * 2
        args += [rope[0], rope[1]]
    out_shape, out_specs = [], []
    for (c0, c1, dtypes) in groups:
        for dt in dtypes:
            out_shape.append(jax.ShapeDtypeStruct((M, c1 - c0), dt))
            out_specs.append(pl.BlockSpec((tm, c1 - c0), lambda i: (i, 0)))
    return pl.pallas_call(
        functools.partial(_norm_proj_kernel, groups=groups, rope_cols=rope_cols),
        grid=(M // tm,), in_specs=in_specs, out_specs=out_specs, out_shape=out_shape,
        compiler_params=_cparams(("parallel",)), name="norm_proj")(*args)


def _sb_kernel(q_ref, k_ref, v_ref, o_ref, *, tq, tk, q_start, k_first):
    i = pl.program_id(2)
    q0 = q_start + i * tq
    q = q_ref[0]
    lane_q = lax.broadcasted_iota(jnp.int32, q.shape, 1)
    zero_q = jnp.zeros_like(q)
    q_heads = (jnp.where(lane_q < SB_DIM, q, zero_q), jnp.where(lane_q >= SB_DIM, q, zero_q))
    qpos = q0 + lax.broadcasted_iota(jnp.int32, (tq, tk), 0)
    col = lax.broadcasted_iota(jnp.int32, (tq, tk), 1)
    r = lax.broadcasted_iota(jnp.int32, (tk, tk), 0)
    c = lax.broadcasted_iota(jnp.int32, (tk, tk), 1)
    later_mat = jnp.where(r > c, 1.0, 0.0).astype(BF16)
    j_first = (q0 + tq - 2) // tk

    def cond(carry):
        j, alive = carry[0], carry[1]
        return jnp.logical_and(j >= 0, alive > 0)

    def body(carry):
        j, _, run_a, run_b, acc_a, acc_b = carry
        k0 = pl.multiple_of(j * tk, tk)
        kt = k_ref[0, pl.ds(k0, tk), :]
        vt = v_ref[0, pl.ds(k0, tk), :]
        kpos = k0 + col
        vis = jnp.logical_and(kpos < qpos, kpos >= k_first)
        new = []
        for qh, run, acc in ((q_heads[0], run_a, acc_a), (q_heads[1], run_b, acc_b)):
            z = lax.dot_general(qh, kt, (((1,), (1,)), ((), ())), preferred_element_type=F32)
            lk0 = -(jnp.maximum(z, 0.0) + jnp.log(1.0 + jnp.exp(-jnp.abs(z))))
            lk = jnp.where(vis, lk0, 0.0)
            later = jnp.dot(lk.astype(BF16), later_mat, preferred_element_type=F32) + run
            w = jnp.where(vis, jnp.exp(z + lk0 + later), 0.0)
            acc = acc + jnp.dot(w.astype(BF16), vt, preferred_element_type=F32)
            run = run + jnp.sum(lk, axis=-1, keepdims=True)
            new += [run, acc]
        alive = (jnp.max(jnp.maximum(new[0], new[2])) > -SB_EXIT).astype(jnp.int32)
        return (j - 1, alive, new[0], new[2], new[1], new[3])

    zr = jnp.zeros((tq, 1), F32)
    za = jnp.zeros((tq, LANES), F32)
    out = lax.while_loop(cond, body, (j_first, jnp.int32(1), zr, zr, za, za))
    lane_o = lax.broadcasted_iota(jnp.int32, (tq, LANES), 1)
    o_ref[0] = jnp.where(lane_o < SB_DIM, out[4], out[5]).astype(o_ref.dtype)


def sb_attention(q, k, v, k_blk, v_blk, *, tq, tk, q_start, k_first):
    B, Lq, _ = q.shape
    Lk = k.shape[1]
    assert Lq % tq == 0 and Lk % tk == 0
    npair = SB_W // LANES
    return pl.pallas_call(
        functools.partial(_sb_kernel, tq=tq, tk=tk, q_start=q_start, k_first=k_first),
        grid=(B, npair, Lq // tq),
        in_specs=[pl.BlockSpec((1, tq, LANES), lambda b, p, i: (b, i, p)),
                  pl.BlockSpec((1, Lk, LANES), lambda b, p, i: (b, 0, k_blk + p)),
                  pl.BlockSpec((1, Lk, LANES), lambda b, p, i: (b, 0, v_blk + p))],
        out_specs=pl.BlockSpec((1, tq, LANES), lambda b, p, i: (b, i, p)),
        out_shape=jax.ShapeDtypeStruct((B, Lq, SB_W), BF16),
        compiler_params=_cparams(("parallel", "parallel", "arbitrary")), name="sb_attention")(q, k, v)


def _dot_hi(a, b):
    return jnp.dot(a, b, preferred_element_type=F32, precision=HI)


def _gdn_kernel(gin_ref, cbuf_ref, s0_ref, cw_ref, alog_ref, dt_ref, alogc_ref, dtc_ref, gn_ref,
                o_ref, s_out_ref, xbuf, state, *, first_valid):
    c = pl.program_id(1)
    C = CHUNK

    @pl.when(c == 0)
    def _():
        state[...] = s0_ref[0]
        xbuf[0:8, :] = cbuf_ref[0]

    xbuf[8:8 + C, :] = gin_ref[0, :, 0:GDN_CONV_CH]
    y = xbuf[8:8 + C, :] * cw_ref[CONV_W - 1:CONV_W, :]
    for d in range(1, CONV_W):
        y = y + xbuf[8 - d:8 - d + C, :] * cw_ref[CONV_W - 1 - d:CONV_W - d, :]
    xbuf[0:8, :] = xbuf[C:C + 8, :]
    conv = y * jax.nn.sigmoid(y)

    ba = gin_ref[0, :, GDN_CONV_CH + GDN_V:GDN_IN_W]
    row_t = c * C + lax.broadcasted_iota(jnp.int32, (C, LANES), 0)
    beta_col = jnp.where(row_t >= first_valid, jax.nn.sigmoid(ba), 0.0)
    g_col = jnp.where(row_t >= first_valid,
                      -jnp.exp(alog_ref[...]) * jax.nn.softplus(ba + dt_ref[...]), 0.0)
    ba_t = ba.T
    col_t = c * C + lax.broadcasted_iota(jnp.int32, (8, C), 1)
    g_row = jnp.where(col_t >= first_valid,
                      -jnp.exp(alogc_ref[...]) * jax.nn.softplus(ba_t[0:8, :] + dtc_ref[...]), 0.0)
    ri = lax.broadcasted_iota(jnp.int32, (C, C), 0)
    ci = lax.broadcasted_iota(jnp.int32, (C, C), 1)
    incl = ri >= ci
    strict = ri > ci
    tril = jnp.where(incl, 1.0, 0.0)
    triu = jnp.where(ri <= ci, 1.0, 0.0)
    eye = jnp.where(ri == ci, 1.0, 0.0)
    G_col = _dot_hi(tril, g_col)
    G_row = _dot_hi(g_row, triu)

    for h in range(GDN_HEADS):
        sl = slice(h * GDN_DK, (h + 1) * GDN_DK)
        qh = conv[:, sl]
        kh = conv[:, GDN_QK + h * GDN_DK:GDN_QK + (h + 1) * GDN_DK]
        vh = conv[:, 2 * GDN_QK + h * GDN_DV:2 * GDN_QK + (h + 1) * GDN_DV]
        qh = qh * lax.rsqrt(jnp.sum(qh * qh, axis=-1, keepdims=True) + L2_EPS) * (GDN_DK ** -0.5)
        kh = kh * lax.rsqrt(jnp.sum(kh * kh, axis=-1, keepdims=True) + L2_EPS)
        gc = G_col[:, GDN_HEADS + h:GDN_HEADS + h + 1]
        gr = G_row[GDN_HEADS + h:GDN_HEADS + h + 1, :]
        bc = beta_col[:, h:h + 1]
        g_last = gc[C - 1:C, :]
        decay = jnp.where(incl, jnp.exp(jnp.minimum(gc - gr, 0.0)), 0.0)
        kk = lax.dot_general(kh, kh, (((1,), (1,)), ((), ())), preferred_element_type=F32, precision=HI)
        qk = lax.dot_general(qh, kh, (((1,), (1,)), ((), ())), preferred_element_type=F32, precision=HI)
        qk = qk * decay
        pw = -jnp.where(strict, bc * kk * decay, 0.0)
        sol = jnp.concatenate([bc * vh, (bc * jnp.exp(gc)) * kh], axis=1)
        steps = int(math.log2(C))
        for s in range(steps):
            sol = sol + _dot_hi(pw, sol)
            if s + 1 < steps:
                pw = _dot_hi(pw, pw)
        u_v = sol[:, :GDN_DV]
        w_k = sol[:, GDN_DV:]
        S = state[h]
        u = u_v - _dot_hi(w_k, S)
        o = _dot_hi(qh * jnp.exp(gc), S) + _dot_hi(qk, u)
        k_tail = kh * jnp.exp(g_last - gc)
        state[h] = jnp.exp(g_last) * S + _dot_hi(k_tail.T, u)
        o = o * lax.rsqrt(jnp.mean(o * o, axis=-1, keepdims=True) + RMS_EPS) * gn_ref[...]
        zh = gin_ref[0, :, GDN_CONV_CH + h * GDN_DV:GDN_CONV_CH + (h + 1) * GDN_DV]
        o_ref[0, :, h * GDN_DV:(h + 1) * GDN_DV] = (o * (zh * jax.nn.sigmoid(zh))).astype(o_ref.dtype)

    @pl.when(c == pl.num_programs(1) - 1)
    def _():
        s_out_ref[0] = state[...]


def gdn_mixer(gin, conv_buf8, s0, conv_w, a_log, dt_bias, gnorm, *, first_valid):
    B, L, _ = gin.shape
    assert L % CHUNK == 0
    pad = jnp.zeros((LANES - 2 * GDN_HEADS,), F32)
    alog_row = jnp.concatenate([jnp.zeros((GDN_HEADS,), F32), a_log.astype(F32), pad])[None, :]
    dt_row = jnp.concatenate([jnp.zeros((GDN_HEADS,), F32), dt_bias.astype(F32), pad])[None, :]
    alog_col = alog_row[0, :8][:, None]
    dt_col = dt_row[0, :8][:, None]
    return pl.pallas_call(
        functools.partial(_gdn_kernel, first_valid=first_valid),
        grid=(B, L // CHUNK),
        in_specs=[pl.BlockSpec((1, CHUNK, GDN_IN_W), lambda b, c: (b, c, 0)),
                  pl.BlockSpec((1, 8, GDN_CONV_CH), lambda b, c: (b, 0, 0)),
                  pl.BlockSpec((1, GDN_HEADS, GDN_DK, GDN_DV), lambda b, c: (b, 0, 0, 0)),
                  pl.BlockSpec((CONV_W, GDN_CONV_CH), lambda b, c: (0, 0)),
                  pl.BlockSpec((1, LANES), lambda b, c: (0, 0)),
                  pl.BlockSpec((1, LANES), lambda b, c: (0, 0)),
                  pl.BlockSpec((8, 1), lambda b, c: (0, 0)),
                  pl.BlockSpec((8, 1), lambda b, c: (0, 0)),
                  pl.BlockSpec((1, GDN_DV), lambda b, c: (0, 0))],
        out_specs=[pl.BlockSpec((1, CHUNK, GDN_V), lambda b, c: (b, c, 0)),
                   pl.BlockSpec((1, GDN_HEADS, GDN_DK, GDN_DV), lambda b, c: (b, 0, 0, 0))],
        out_shape=[jax.ShapeDtypeStruct((B, L, GDN_V), BF16),
                   jax.ShapeDtypeStruct((B, GDN_HEADS, GDN_DK, GDN_DV), F32)],
        scratch_shapes=[pltpu.VMEM((CHUNK + 8, GDN_CONV_CH), F32),
                        pltpu.VMEM((GDN_HEADS, GDN_DK, GDN_DV), F32)],
        compiler_params=_cparams(("parallel", "arbitrary")), name="gdn_mixer")(
            gin, conv_buf8, s0, conv_w.astype(F32), alog_row, dt_row, alog_col, dt_col,
            gnorm.astype(F32)[None, :])


def _mix_ffn_kernel(h_ref, oa_ref, ob_ref, valid_ref, woa_ref, wob_ref, g_ref, w1_ref, w3_ref, w2_ref,
                    out_ref):
    h1 = (h_ref[...] + jnp.dot(oa_ref[...], woa_ref[...], preferred_element_type=F32)
          + jnp.dot(ob_ref[...], wob_ref[...], preferred_element_type=F32))
    u = _rms(h1, g_ref[...]).astype(BF16)
    out_ref[...] = h1
    for (a, b) in _col_chunks(0, D_FF):
        x1 = jnp.dot(u, w1_ref[:, a:b], preferred_element_type=F32)
        x3 = jnp.dot(u, w3_ref[:, a:b], preferred_element_type=F32)
        hid = (x1 * jax.nn.sigmoid(x1) * x3).astype(BF16)
        out_ref[...] += jnp.dot(hid, w2_ref[a:b, :], preferred_element_type=F32)
    out_ref[...] = jnp.where(valid_ref[...] > 0.0, out_ref[...], 0.0)


def mix_ffn(h, oa, ob, valid, woa, wob, g, w1, w3, w2, tm):
    M, D = h.shape
    assert M % tm == 0
    row = lambda i: (i, 0)
    return pl.pallas_call(
        _mix_ffn_kernel, grid=(M // tm,),
        in_specs=[pl.BlockSpec((tm, D), row), pl.BlockSpec((tm, SB_W), row), pl.BlockSpec((tm, GDN_V), row),
                  pl.BlockSpec((tm, 1), row), _const_spec(woa.shape), _const_spec(wob.shape),
                  _const_spec((1, D)), _const_spec(w1.shape), _const_spec(w3.shape), _const_spec(w2.shape)],
        out_specs=pl.BlockSpec((tm, D), row),
        out_shape=jax.ShapeDtypeStruct((M, D), F32),
        compiler_params=_cparams(("parallel",)), name="mix_ffn")(h, oa, ob, valid, woa, wob, g, w1, w3, w2)


def _swa_kernel(q_ref, k0_ref, k1_ref, k2_ref, v0_ref, v1_ref, v2_ref, sink_ref, o_ref, *,
                chunk_offset, k_first):
    c = pl.program_id(1) + chunk_offset
    C = CHUNK
    nk = (WIN_CHUNKS + 1) * C
    k = jnp.concatenate([k0_ref[0], k1_ref[0], k2_ref[0]], axis=0)
    v = jnp.concatenate([v0_ref[0], v1_ref[0], v2_ref[0]], axis=0)
    lane = lax.broadcasted_iota(jnp.int32, (nk, LANES), 1)
    lo = lane < SW_DIM
    k_sw = pltpu.roll(k, SW_DIM, 1)
    v_sw = pltpu.roll(v, SW_DIM, 1)
    key_pos = (c - WIN_CHUNKS) * C + lax.broadcasted_iota(jnp.int32, (1, nk), 1)
    key_ok = key_pos >= k_first
    pairs = SW_GROUP // 2
    for g in range(SW_KV_HEADS):
        ksrc_a, ksrc_b = (k, k_sw) if g == 0 else (k_sw, k)
        vsrc_a, vsrc_b = (v, v_sw) if g == 0 else (v_sw, v)
        k_a = jnp.where(lo, ksrc_a, 0.0).astype(BF16)
        k_b = jnp.where(lo, 0.0, ksrc_b).astype(BF16)
        v_a = jnp.where(lo, vsrc_a, 0.0).astype(BF16)
        v_b = jnp.where(lo, 0.0, vsrc_b).astype(BF16)
        qs = jnp.concatenate([q_ref[0, :, (g * pairs + j) * LANES:(g * pairs + j + 1) * LANES]
                              for j in range(pairs)], axis=0)
        acc = None
        for t, (kx, vx) in enumerate(((k_a, v_a), (k_b, v_b))):
            s = lax.dot_general(qs, kx, (((1,), (1,)), ((), ())), preferred_element_type=F32)
            s = jnp.where(key_ok, s, NEG_BIG)
            snk = sink_ref[2 * g + t]
            m = jnp.maximum(jnp.max(s, axis=-1, keepdims=True), snk)
            p = jnp.exp(s - m)
            den = jnp.sum(p, axis=-1, keepdims=True) + jnp.exp(snk - m)
            p = (p / den).astype(BF16)
            pv = jnp.dot(p, vx, preferred_element_type=F32)
            acc = pv if acc is None else acc + pv
        for j in range(pairs):
            o_ref[0, :, (g * pairs + j) * LANES:(g * pairs + j + 1) * LANES] = (
                acc[j * C:(j + 1) * C, :].astype(o_ref.dtype))


def swa_attention(q, kv, sinks, *, n_chunks, chunk_offset, k_first):
    B = q.shape[0]
    pairs = SW_GROUP // 2
    sk = sinks.astype(F32).reshape(SW_KV_HEADS, pairs, 2).transpose(0, 2, 1)
    sk = jnp.repeat(sk.reshape(SW_KV_HEADS * 2, pairs), CHUNK, axis=1)[:, :, None]

    def kspec(d, blk):
        return pl.BlockSpec((1, CHUNK, LANES),
                            lambda b, c: (b, jnp.maximum(c + chunk_offset - d, 0), blk))

    return pl.pallas_call(
        functools.partial(_swa_kernel, chunk_offset=chunk_offset, k_first=k_first),
        grid=(B, n_chunks),
        in_specs=[pl.BlockSpec((1, CHUNK, SW_HEADS * SW_DIM), lambda b, c: (b, c, 0)),
                  kspec(2, 0), kspec(1, 0), kspec(0, 0), kspec(2, 1), kspec(1, 1), kspec(0, 1),
                  pl.BlockSpec((2 * SW_KV_HEADS, pairs * CHUNK, 1), lambda b, c: (0, 0, 0))],
        out_specs=pl.BlockSpec((1, CHUNK, SW_HEADS * SW_DIM), lambda b, c: (b, c, 0)),
        out_shape=jax.ShapeDtypeStruct((B, n_chunks * CHUNK, SW_HEADS * SW_DIM), BF16),
        compiler_params=_cparams(("parallel", "parallel")), name="swa_attention")(
            q, kv, kv, kv, kv, kv, kv, sk)


def _mix_router_kernel(h_ref, o_ref, wo_ref, g_ref, rw_ref, rb_ref, h1_ref, u_ref, lg_ref):
    h1 = h_ref[...] + jnp.dot(o_ref[...], wo_ref[...], preferred_element_type=F32)
    h1_ref[...] = h1
    u = _rms(h1, g_ref[...])
    u_ref[...] = u.astype(u_ref.dtype)
    lg_ref[...] = _dot_hi(u, rw_ref[...]) + rb_ref[...]


def mix_router(h, o, wo, g, rw, rb, tm):
    M, D = h.shape
    assert M % tm == 0
    row = lambda i: (i, 0)
    return pl.pallas_call(
        _mix_router_kernel, grid=(M // tm,),
        in_specs=[pl.BlockSpec((tm, D), row), pl.BlockSpec((tm, D), row), _const_spec(wo.shape),
                  _const_spec((1, D)), _const_spec(rw.shape), _const_spec(rb.shape)],
        out_specs=[pl.BlockSpec((tm, D), row), pl.BlockSpec((tm, D), row), pl.BlockSpec((tm, LANES), row)],
        out_shape=[jax.ShapeDtypeStruct((M, D), F32), jax.ShapeDtypeStruct((M, D), BF16),
                   jax.ShapeDtypeStruct((M, LANES), F32)],
        compiler_params=_cparams(("parallel",)), name="mix_router")(h, o, wo, g, rw, rb)


def _expert_kernel(be_ref, nu_ref, x_ref, gate_ref, w1_ref, w3_ref, w2_ref, out_ref):
    i = pl.program_id(0)

    @pl.when(i < nu_ref[0])
    def _():
        x = x_ref[...]
        for (a, b) in _col_chunks(0, EXPERT_FF):
            x1 = jnp.dot(x, w1_ref[0, :, a:b], preferred_element_type=F32)
            x3 = jnp.dot(x, w3_ref[0, :, a:b], preferred_element_type=F32)
            hid = (x1 * jax.nn.sigmoid(x1) * x3).astype(BF16)
            y = jnp.dot(hid, w2_ref[0, a:b, :], preferred_element_type=F32)
            if a == 0:
                out_ref[...] = y
            else:
                out_ref[...] += y
        out_ref[...] = out_ref[...] * gate_ref[...]

    @pl.when(i >= nu_ref[0])
    def _():
        out_ref[...] = jnp.zeros(out_ref.shape, F32)


def expert_ffn(blk_expert, n_used, xs, gates, w1, w3, w2, blk):
    n_slot, D = xs.shape
    n_blk = n_slot // blk
    wspec = lambda shape: pl.BlockSpec((1,) + shape, lambda i, be, nu: (be[i], 0, 0),
                                       pipeline_mode=pl.Buffered(1))
    return pl.pallas_call(
        _expert_kernel,
        grid_spec=pltpu.PrefetchScalarGridSpec(
            num_scalar_prefetch=2, grid=(n_blk,),
            in_specs=[pl.BlockSpec((blk, D), lambda i, be, nu: (i, 0)),
                      pl.BlockSpec((blk, 1), lambda i, be, nu: (i, 0)),
                      wspec((D, EXPERT_FF)), wspec((D, EXPERT_FF)), wspec((EXPERT_FF, D))],
            out_specs=pl.BlockSpec((blk, D), lambda i, be, nu: (i, 0))),
        out_shape=jax.ShapeDtypeStruct((n_slot, D), F32),
        compiler_params=_cparams(("arbitrary",)), name="expert_ffn")(
            blk_expert, n_used, xs, gates, w1, w3, w2)


def _final_kernel(h_ref, y_ref, g_ref, out_ref):
    out_ref[0] = _rms(h_ref[0] + y_ref[0], g_ref[...])


def final_norm(h, y, g, *, skip_rows, tm):
    B, L, D = h.shape
    assert skip_rows % tm == 0 and (L - skip_rows) % tm == 0
    off = skip_rows // tm
    return pl.pallas_call(
        _final_kernel, grid=(B, (L - skip_rows) // tm),
        in_specs=[pl.BlockSpec((1, tm, D), lambda b, i: (b, i + off, 0)),
                  pl.BlockSpec((1, tm, D), lambda b, i: (b, i + off, 0)),
                  pl.BlockSpec((1, D), lambda b, i: (0, 0))],
        out_specs=pl.BlockSpec((1, tm, D), lambda b, i: (b, i, 0)),
        out_shape=jax.ShapeDtypeStruct((B, L - skip_rows, D), F32),
        compiler_params=_cparams(("parallel", "parallel")), name="final_norm")(h, y, g)


def _route(logits, blk):
    n_tok = logits.shape[0]
    n_asg = n_tok * TOP_K
    top_val, top_idx = lax.top_k(logits, TOP_K)
    gates = jax.nn.softmax(top_val, axis=-1).reshape(-1)
    expert = top_idx.reshape(-1).astype(jnp.int32)
    onehot = (expert[:, None] == jnp.arange(N_EXPERTS, dtype=jnp.int32)[None, :]).astype(jnp.int32)
    csum = jnp.cumsum(onehot, axis=0)
    counts = csum[-1]
    rank = jnp.take_along_axis(csum, expert[:, None], axis=1)[:, 0] - 1
    padded = (counts + blk - 1) // blk * blk
    pend = jnp.cumsum(padded)
    pstart = pend - padded
    dest = pstart[expert] + rank
    n_blk = (n_asg + N_EXPERTS * (blk - 1) + blk - 1) // blk
    n_slot = n_blk * blk
    token = jnp.arange(n_asg, dtype=jnp.int32) // TOP_K
    slot_tok = jnp.full((n_slot,), n_tok, jnp.int32).at[dest].set(token)
    slot_gate = jnp.zeros((n_slot,), F32).at[dest].set(gates)
    blk_expert = jnp.minimum(jnp.searchsorted(pend, jnp.arange(n_blk, dtype=jnp.int32) * blk, side='right'),
                             N_EXPERTS - 1).astype(jnp.int32)
    n_used = (pend[-1] // blk).astype(jnp.int32).reshape(1)
    return slot_tok, slot_gate, blk_expert, n_used, dest.reshape(n_tok, TOP_K)


def _moe(u_bf, logits, w1, w3, w2, blk):
    n_tok = u_bf.shape[0]
    slot_tok, slot_gate, blk_expert, n_used, dest = _route(logits[:, :N_EXPERTS], blk)
    xs_pad = jnp.concatenate([u_bf, jnp.zeros((1, D_MODEL), u_bf.dtype)], axis=0)
    xs = xs_pad[slot_tok]
    yb = expert_ffn(blk_expert, n_used, xs, slot_gate[:, None], w1, w3, w2, blk)
    return yb[dest[:, 0]] + yb[dest[:, 1]]


def _rope_tables(pos):
    half = SW_DIM // 2
    inv_freq = ROPE_THETA ** (-jnp.arange(half, dtype=F32) / half)
    ang = pos.astype(F32)[:, None] * inv_freq[None, :]
    cos, sin = jnp.cos(ang), jnp.sin(ang)
    cos = jnp.concatenate([cos, cos, cos, cos], axis=1)
    sin = jnp.concatenate([-sin, sin, -sin, sin], axis=1)
    return cos, sin


def _row_tile(m, pref):
    t = pref
    while m % t:
        t //= 2
    return t


def kernel(x_prompt, x_sample, cache_sb_k, cache_sb_v, state_gdn, state_gdn_conv, cache_sw_k, cache_sw_v, meta_tokens, norm_mix, norm_ffn, norm_final, w_in_even, gdn_conv_w, gdn_a_log, gdn_dt_bias, gdn_norm, w_out_even, ffn_w1, ffn_w3, ffn_w2, w_in_odd, sw_sinks, w_out_odd, router_w, router_b, moe_w1, moe_w3, moe_w2):
    nb, seq, D = x_prompt.shape
    ns, tn, _ = x_sample.shape
    past = cache_sb_k.shape[1]
    lp = PREFIX + seq

    c_q, c_k, c_v, c_x, c_b, c_a, c_z = np.cumsum((0, SB_W, SB_W, SB_W, GDN_CONV_CH, GDN_HEADS, GDN_HEADS)).tolist()
    w_e = w_in_even.astype(F32)
    w_even = jnp.concatenate([
        w_e[:, c_q:c_k] * (SB_DIM ** -0.5), w_e[:, c_k:c_x], w_e[:, c_x:c_b], w_e[:, c_z:c_z + GDN_V],
        w_e[:, c_b:c_z], jnp.zeros((D, LANES - 2 * GDN_HEADS), F32)], axis=1).astype(BF16)
    qkv_w = 3 * SB_W
    even_groups = ((0, SB_W, (BF16,)), (SB_W, qkv_w, (BF16, F32)), (qkv_w, qkv_w + GDN_IN_W, (F32,)))
    w_o = w_in_odd.astype(F32)
    q_w = SW_HEADS * SW_DIM
    kv_w = SW_KV_HEADS * SW_DIM
    w_odd = jnp.concatenate([w_o[:, :q_w] * (SW_DIM ** -0.5), w_o[:, q_w:]], axis=1).astype(BF16)
    odd_groups = ((0, q_w, (BF16,)), (q_w, q_w + 2 * kv_w, (F32,)))
    woa = w_out_even[:SB_W].astype(BF16)
    wob = w_out_even[SB_W:].astype(BF16)
    w1, w3, w2 = ffn_w1.astype(BF16), ffn_w3.astype(BF16), ffn_w2.astype(BF16)
    wo_odd = w_out_odd.astype(BF16)
    rw = jnp.concatenate([router_w.astype(F32), jnp.zeros((D, LANES - N_EXPERTS), F32)], axis=1)
    rb = jnp.concatenate([router_b.astype(F32), jnp.zeros((LANES - N_EXPERTS,), F32)])[None, :]
    mw1, mw3, mw2 = moe_w1.astype(BF16), moe_w3.astype(BF16), moe_w2.astype(BF16)
    g_mix = norm_mix.astype(F32)
    g_ffn = norm_ffn.astype(F32)
    g_fin = norm_final.astype(F32)[None, :]

    def layers(h, B, L, first_valid, sb_fn, conv_buf, s0, pos, sw_fn, moe_blk):
        M = B * L
        tm = _row_tile(M, 512)
        valid = (jnp.arange(L) >= first_valid).astype(F32)
        valid = jnp.broadcast_to(valid[None, :], (B, L)).reshape(M, 1)
        q_bf, kv_bf, kv_f32, gin = norm_proj(h, g_mix[0:1], w_even, even_groups, tm)
        oa = sb_fn(q_bf.reshape(B, L, SB_W), kv_bf.reshape(B, L, 2 * SB_W))
        conv8 = jnp.concatenate([jnp.zeros((B, 8 - (CONV_W - 1), GDN_CONV_CH), F32), conv_buf.astype(F32)], axis=1)
        gin3 = gin.reshape(B, L, GDN_IN_W)
        ob, s_new = gdn_mixer(gin3, conv8, s0.astype(F32), gdn_conv_w, gdn_a_log, gdn_dt_bias, gdn_norm,
                              first_valid=first_valid)
        conv_new = jnp.concatenate([conv_buf.astype(F32), gin3[:, :, :GDN_CONV_CH]], axis=1)[:, -(CONV_W - 1):]
        h = mix_ffn(h, oa.reshape(M, SB_W), ob.reshape(M, GDN_V), valid, woa, wob, g_ffn[0:1], w1, w3, w2, tm)
        cos, sin = _rope_tables(jnp.broadcast_to(pos[None, :], (B, L)).reshape(M))
        q_sw, kv_sw = norm_proj(h, g_mix[1:2], w_odd, odd_groups, tm, rope=(cos, sin, q_w + kv_w))
        kv_sw = kv_sw.reshape(B, L, 2 * kv_w)
        o_sw = sw_fn(q_sw.reshape(B, L, q_w), kv_sw)
        h1, u_bf, logits = mix_router(h, o_sw.reshape(M, q_w), wo_odd, g_ffn[1:2], rw, rb, tm)
        y = _moe(u_bf, logits, mw1, mw3, mw2, moe_blk)
        kv4 = kv_f32.reshape(B, L, 2, SB_HEADS, SB_DIM)
        return h1.reshape(B, L, D), y.reshape(B, L, D), kv4[:, :, 0], kv4[:, :, 1], s_new, conv_new, kv_sw

    dt = x_prompt.dtype
    hp = jnp.concatenate([jnp.zeros((nb, FIRST_VALID, D), dt),
                          jnp.broadcast_to(meta_tokens.astype(dt), (nb, N_META_TOK, D)), x_prompt], axis=1)
    pos_p = jnp.arange(lp, dtype=jnp.int32) - FIRST_VALID

    def sb_prompt(q, kv):
        return sb_attention(q, kv, kv, 0, SB_W // LANES, tq=128, tk=128, q_start=0, k_first=FIRST_VALID)

    def sw_prompt(q, kv):
        return swa_attention(q, kv, sw_sinks, n_chunks=lp // CHUNK, chunk_offset=0, k_first=FIRST_VALID)

    h1p, yp, sbk_p, sbv_p, gdn_p, conv_p, kvsw_p = layers(
        hp.reshape(nb * lp, D), nb, lp, FIRST_VALID, sb_prompt,
        jnp.zeros((nb, CONV_W - 1, GDN_CONV_CH), dt), jnp.zeros((nb, GDN_HEADS, GDN_DK, GDN_DV), dt),
        pos_p, sw_prompt, 512)
    y_prompt = final_norm(h1p, yp, g_fin, skip_rows=PREFIX, tm=128)

    pos_s = N_META_TOK + past + jnp.arange(tn, dtype=jnp.int32)

    def sb_samp(q, kv):
        kk = jnp.concatenate([cache_sb_k.reshape(ns, past, SB_W).astype(BF16), kv[:, :, :SB_W]], axis=1)
        vv = jnp.concatenate([cache_sb_v.reshape(ns, past, SB_W).astype(BF16), kv[:, :, SB_W:]], axis=1)
        return sb_attention(q, kk, vv, 0, 0, tq=tn, tk=64, q_start=past, k_first=0)

    def sw_samp(q, kv):
        kc = cache_sw_k.reshape(ns, -1, kv_w).astype(F32)
        vc = cache_sw_v.reshape(ns, -1, kv_w).astype(F32)
        kv_all = jnp.concatenate([jnp.concatenate([kc, vc], axis=2), kv], axis=1)
        return swa_attention(q, kv_all, sw_sinks, n_chunks=1, chunk_offset=WIN_CHUNKS, k_first=0)

    h1s, ys, sbk_s, sbv_s, gdn_s, conv_s, kvsw_s = layers(
        x_sample.reshape(ns * tn, D), ns, tn, 0, sb_samp, state_gdn_conv, state_gdn, pos_s, sw_samp, 128)
    y_sample = final_norm(h1s, ys, g_fin, skip_rows=0, tm=tn)

    def sw_split(kv):
        return (kv[:, :, :kv_w].reshape(kv.shape[0], -1, SW_KV_HEADS, SW_DIM),
                kv[:, :, kv_w:].reshape(kv.shape[0], -1, SW_KV_HEADS, SW_DIM))

    swk_p, swv_p = sw_split(kvsw_p[:, -WINDOW:])
    swk_s, swv_s = sw_split(kvsw_s)
    buf = cache_sw_k.shape[1]
    sw_k_s = jnp.concatenate([cache_sw_k, swk_s.astype(cache_sw_k.dtype)], axis=1)[:, -buf:]
    sw_v_s = jnp.concatenate([cache_sw_v, swv_s.astype(cache_sw_v.dtype)], axis=1)[:, -buf:]
    return (y_prompt, y_sample,
            sbk_p[:, FIRST_VALID:], sbv_p[:, FIRST_VALID:], sbk_s, sbv_s,
            gdn_p.astype(dt), gdn_s.astype(state_gdn.dtype), conv_p.astype(dt), conv_s.astype(dt),
            swk_p, swv_p, sw_k_s, sw_v_s)
```

```python
import functools
import math

import numpy as np
import jax
import jax.numpy as jnp
from jax import lax
from jax.experimental import pallas as pl
from jax.experimental.pallas import tpu as pltpu

F32 = jnp.float32
BF16 = jnp.bfloat16
HI = lax.Precision.HIGHEST

D_MODEL = 1024
CHUNK = 64
N_META_TOK = 16
PREFIX = 128
FIRST_VALID = PREFIX - N_META_TOK
RMS_EPS = 1e-6
L2_EPS = 1e-6

SB_HEADS = 8
SB_DIM = 64
SB_W = SB_HEADS * SB_DIM
GDN_HEADS = 4
GDN_DK = 128
GDN_DV = 128
CONV_W = 4
GDN_QK = GDN_HEADS * GDN_DK
GDN_V = GDN_HEADS * GDN_DV
GDN_CONV_CH = 2 * GDN_QK + GDN_V
GDN_IN_W = GDN_CONV_CH + GDN_V + 128
SW_HEADS = 16
SW_KV_HEADS = 2
SW_GROUP = SW_HEADS // SW_KV_HEADS
SW_DIM = 64
WINDOW = 128
WIN_CHUNKS = WINDOW // CHUNK
ROPE_THETA = 10000.0
D_FF = 2816
N_EXPERTS = 8
TOP_K = 2
EXPERT_FF = 3584

LANES = 128
VMEM_LIMIT = 56 * 1024 * 1024
NEG_BIG = -1e30
SB_EXIT = 120.0


def _cparams(sem):
    return pltpu.CompilerParams(dimension_semantics=sem, vmem_limit_bytes=VMEM_LIMIT)


def _const_spec(shape):
    nd = len(shape)
    return pl.BlockSpec(shape, lambda *_: (0,) * nd, pipeline_mode=pl.Buffered(1))


def _rms(x, g):
    ms = jnp.mean(x * x, axis=-1, keepdims=True)
    return x * lax.rsqrt(ms + RMS_EPS) * g


def _col_chunks(c0, c1, step=512):
    out = []
    while c0 < c1:
        out.append((c0, min(c0 + step, c1)))
        c0 = out[-1][1]
    return out


def _norm_proj_kernel(*refs, groups, rope_cols):
    if rope_cols:
        x_ref, g_ref, w_ref, cos_ref, sin_ref = refs[:5]
        outs = refs[5:]
    else:
        x_ref, g_ref, w_ref = refs[:3]
        outs = refs[3:]
    u = _rms(x_ref[...], g_ref[...]).astype(BF16)
    if rope_cols:
        cos = cos_ref[...]
        sin = sin_ref[...]
        lane = lax.broadcasted_iota(jnp.int32, cos.shape, 1)
        first_half = (lane % SW_DIM) < (SW_DIM // 2)
    oi = 0
    for (c0, c1, dtypes) in groups:
        for (a, b) in _col_chunks(c0, c1):
            y = jnp.dot(u, w_ref[:, a:b], preferred_element_type=F32)
            if a < rope_cols:
                pieces = []
                for p in range(0, b - a, LANES):
                    yp = y[:, p:p + LANES]
                    if a + p < rope_cols:
                        rot = jnp.where(first_half, pltpu.roll(yp, LANES - SW_DIM // 2, 1),
                                        pltpu.roll(yp, SW_DIM // 2, 1))
                        yp = yp * cos + rot * sin
                    pieces.append(yp)
                y = jnp.concatenate(pieces, axis=1) if len(pieces) > 1 else pieces[0]
            for k, dt in enumerate(dtypes):
                outs[oi + k][:, a - c0:b - c0] = y.astype(dt)
        oi += len(dtypes)


def norm_proj(x, g, w, groups, tm, rope=None):
    M, D = x.shape
    N = w.shape[1]
    assert M % tm == 0
    rope_cols = rope[2] if rope is not None else 0
    in_specs = [pl.BlockSpec((tm, D), lambda i: (i, 0)), _const_spec((1, D)), _const_spec((D, N))]
    args = [x, g, w]
    if rope is not None:
        in_specs += [pl.BlockSpec((tm, LANES), lambda i: (i, 0))] * 2
        args += [rope[0], rope[1]]
    out_shape, out_specs = [], []
    for (c0, c1, dtypes) in groups:
        for dt in dtypes:
            out_shape.append(jax.ShapeDtypeStruct((M, c1 - c0), dt))
            out_specs.append(pl.BlockSpec((tm, c1 - c0), lambda i: (i, 0)))
    return pl.pallas_call(
        functools.partial(_norm_proj_kernel, groups=groups, rope_cols=rope_cols),
        grid=(M // tm,), in_specs=in_specs, out_specs=out_specs, out_shape=out_shape,
        compiler_params=_cparams(("parallel",)), name="norm_proj")(*args)


SB_HB = 2


def _sb_kernel(q_ref, k_ref, v_ref, o_ref, run_ref, acc_ref, *, tq, tk, q_start, k_first):
    i = pl.program_id(2)
    q0 = q_start + i * tq
    align = math.gcd(math.gcd(tq, tk), q_start) if q_start else math.gcd(tq, tk)
    lane_q = lax.broadcasted_iota(jnp.int32, (tq, LANES), 1)
    q_heads = []
    for lb in range(SB_HB):
        q = q_ref[0, :, lb * LANES:(lb + 1) * LANES]
        zero_q = jnp.zeros_like(q)
        q_heads += [jnp.where(lane_q < SB_DIM, q, zero_q), jnp.where(lane_q >= SB_DIM, q, zero_q)]
    r = lax.broadcasted_iota(jnp.int32, (tk, tk), 0)
    c = lax.broadcasted_iota(jnp.int32, (tk, tk), 1)
    later_mat = jnp.where(r > c, 1.0, 0.0).astype(BF16)
    run_ref[...] = jnp.zeros(run_ref.shape, F32)
    acc_ref[...] = jnp.zeros(acc_ref.shape, F32)

    def cond(carry):
        _, hi, alive = carry
        return jnp.logical_and(hi > 0, alive > 0)

    def body(carry):
        s, hi, _ = carry
        s = pl.multiple_of(s, align)
        qpos = q0 + lax.broadcasted_iota(jnp.int32, (tq, tk), 0)
        kpos = s + lax.broadcasted_iota(jnp.int32, (tq, tk), 1)
        lim = jnp.minimum(qpos, hi)
        if k_first:
            vis = (pltpu.bitcast(kpos - k_first, jnp.uint32)
                   < pltpu.bitcast(jnp.maximum(lim - k_first, 0), jnp.uint32))
        else:
            vis = kpos < lim
        for hh in range(2 * SB_HB):
            lb = hh // 2
            kt = k_ref[0, pl.ds(s, tk), lb * LANES:(lb + 1) * LANES]
            vt = v_ref[0, pl.ds(s, tk), lb * LANES:(lb + 1) * LANES]
            z = lax.dot_general(q_heads[hh], kt, (((1,), (1,)), ((), ())), preferred_element_type=F32)
            lk0 = -(jnp.maximum(z, 0.0) + jnp.log(1.0 + jnp.exp(-jnp.abs(z))))
            lk = jnp.where(vis, lk0, 0.0)
            later = jnp.dot(lk.astype(BF16), later_mat, preferred_element_type=F32) + run_ref[hh]
            w = jnp.where(vis, jnp.exp(z + lk0 + later), 0.0)
            acc_ref[hh] += jnp.dot(w.astype(BF16), vt, preferred_element_type=F32)
            run_ref[hh] += jnp.sum(lk, axis=-1, keepdims=True)
        top = run_ref[0]
        for hh in range(1, 2 * SB_HB):
            top = jnp.maximum(top, run_ref[hh])
        alive = (jnp.max(top) > -SB_EXIT).astype(jnp.int32)
        return (jnp.maximum(s - tk, 0), s, alive)

    lax.while_loop(cond, body, (jnp.maximum(q0 + tq - tk, 0), q0 + tq, jnp.int32(1)))
    for lb in range(SB_HB):
        o_ref[0, :, lb * LANES:(lb + 1) * LANES] = jnp.where(
            lane_q < SB_DIM, acc_ref[2 * lb], acc_ref[2 * lb + 1]).astype(o_ref.dtype)


def sb_attention(q, k, v, k_off, v_off, *, tq, tk, q_start, k_first):
    B, Lq, _ = q.shape
    Lk = k.shape[1]
    wb = SB_HB * LANES
    assert Lq % tq == 0 and Lk >= tk and (q_start + tq - tk) % 16 == 0 and tk % 16 == 0
    return pl.pallas_call(
        functools.partial(_sb_kernel, tq=tq, tk=tk, q_start=q_start, k_first=k_first),
        grid=(B, SB_W // wb, Lq // tq),
        in_specs=[pl.BlockSpec((1, tq, wb), lambda b, p, i: (b, i, p)),
                  pl.BlockSpec((1, Lk, wb), lambda b, p, i: (b, 0, k_off + p)),
                  pl.BlockSpec((1, Lk, wb), lambda b, p, i: (b, 0, v_off + p))],
        out_specs=pl.BlockSpec((1, tq, wb), lambda b, p, i: (b, i, p)),
        out_shape=jax.ShapeDtypeStruct((B, Lq, SB_W), BF16),
        scratch_shapes=[pltpu.VMEM((2 * SB_HB, tq, 1), F32), pltpu.VMEM((2 * SB_HB, tq, LANES), F32)],
        compiler_params=_cparams(("parallel", "parallel", "arbitrary")), name="sb_attention")(q, k, v)


def _dot_hi(a, b):
    return jnp.dot(a, b, preferred_element_type=F32, precision=HI)


def _dot_bf(a, b):
    return jnp.dot(a.astype(BF16), b.astype(BF16), preferred_element_type=F32)


def _split_bf(a, parts):
    out = []
    for _ in range(parts - 1):
        hi = a.astype(BF16)
        out.append(hi)
        a = a - hi.astype(F32)
    out.append(a.astype(BF16))
    return out


def _dot_split(a_parts, b_parts):
    acc = None
    for i, ap in enumerate(a_parts):
        for j, bp in enumerate(b_parts):
            if i + j < max(len(a_parts), len(b_parts)):
                d = jnp.dot(ap, bp, preferred_element_type=F32)
                acc = d if acc is None else acc + d
    return acc


def _gdn_kernel(gin_ref, cbuf_ref, s0_ref, cw_ref, alog_ref, dt_ref, alogc_ref, dtc_ref, gn_ref,
                o_ref, s_out_ref, xbuf, state, *, first_valid, bb):
    c = pl.program_id(1)
    C = CHUNK

    @pl.when(c == 0)
    def _():
        state[...] = s0_ref[...]
        xbuf[:, 0:8, :] = cbuf_ref[...]

    ri = lax.broadcasted_iota(jnp.int32, (C, C), 0)
    ci = lax.broadcasted_iota(jnp.int32, (C, C), 1)
    incl = ri >= ci
    strict = ri > ci
    tril = jnp.where(incl, 1.0, 0.0).astype(BF16)
    triu = jnp.where(ri <= ci, 1.0, 0.0).astype(BF16)
    row_t = c * C + lax.broadcasted_iota(jnp.int32, (C, LANES), 0)
    col_t = c * C + lax.broadcasted_iota(jnp.int32, (8, C), 1)

    for bi in range(bb):
        xbuf[bi, 8:8 + C, :] = gin_ref[bi, :, 0:GDN_CONV_CH]
        y = xbuf[bi, 8:8 + C, :] * cw_ref[CONV_W - 1:CONV_W, :]
        for d in range(1, CONV_W):
            y = y + xbuf[bi, 8 - d:8 - d + C, :] * cw_ref[CONV_W - 1 - d:CONV_W - d, :]
        xbuf[bi, 0:8, :] = xbuf[bi, C:C + 8, :]
        conv = y * jax.nn.sigmoid(y)

        ba = gin_ref[bi, :, GDN_CONV_CH + GDN_V:GDN_IN_W]
        beta_col = jnp.where(row_t >= first_valid, jax.nn.sigmoid(ba), 0.0)
        g_col = jnp.where(row_t >= first_valid,
                          -jnp.exp(alog_ref[...]) * jax.nn.softplus(ba + dt_ref[...]), 0.0)
        ba_t = ba.T
        g_row = jnp.where(col_t >= first_valid,
                          -jnp.exp(alogc_ref[...]) * jax.nn.softplus(ba_t[0:8, :] + dtc_ref[...]), 0.0)
        G_col = _dot_split([tril], _split_bf(g_col, 3))
        G_row = _dot_split(_split_bf(g_row, 3), [triu])

        for h in range(GDN_HEADS):
            qh = conv[:, h * GDN_DK:(h + 1) * GDN_DK]
            kh = conv[:, GDN_QK + h * GDN_DK:GDN_QK + (h + 1) * GDN_DK]
            vh = conv[:, 2 * GDN_QK + h * GDN_DV:2 * GDN_QK + (h + 1) * GDN_DV]
            qh = qh * lax.rsqrt(jnp.sum(qh * qh, axis=-1, keepdims=True) + L2_EPS) * (GDN_DK ** -0.5)
            kh = kh * lax.rsqrt(jnp.sum(kh * kh, axis=-1, keepdims=True) + L2_EPS)
            gc = G_col[:, GDN_HEADS + h:GDN_HEADS + h + 1]
            gr = G_row[GDN_HEADS + h:GDN_HEADS + h + 1, :]
            bc = beta_col[:, h:h + 1]
            g_last = gc[C - 1:C, :]
            decay = jnp.where(incl, jnp.exp(jnp.minimum(gc - gr, 0.0)), 0.0)
            kh_bf = kh.astype(BF16)
            kk = lax.dot_general(kh_bf, kh_bf, (((1,), (1,)), ((), ())), preferred_element_type=F32)
            qk = lax.dot_general(qh.astype(BF16), kh_bf, (((1,), (1,)), ((), ())),
                                 preferred_element_type=F32) * decay
            pw = -jnp.where(strict, bc * kk * decay, 0.0)
            sol = jnp.concatenate([bc * vh, (bc * jnp.exp(gc)) * kh], axis=1)
            steps = int(math.log2(C))
            for s in range(steps):
                pw_parts = _split_bf(pw, 2)
                sol = sol + _dot_split(pw_parts, _split_bf(sol, 2))
                if s + 1 < steps:
                    pw = _dot_split(pw_parts, pw_parts)
            u_v = sol[:, :GDN_DV]
            w_k = sol[:, GDN_DV:]
            S = state[bi, h]
            S_bf = S.astype(BF16)
            u = u_v - jnp.dot(w_k.astype(BF16), S_bf, preferred_element_type=F32)
            o = (jnp.dot((qh * jnp.exp(gc)).astype(BF16), S_bf, preferred_element_type=F32)
                 + _dot_bf(qk, u))
            k_tail = kh * jnp.exp(g_last - gc)
            state[bi, h] = jnp.exp(g_last) * S + _dot_bf(k_tail.T, u)
            o = o * lax.rsqrt(jnp.mean(o * o, axis=-1, keepdims=True) + RMS_EPS) * gn_ref[...]
            zh = gin_ref[bi, :, GDN_CONV_CH + h * GDN_DV:GDN_CONV_CH + (h + 1) * GDN_DV]
            o_ref[bi, :, h * GDN_DV:(h + 1) * GDN_DV] = (o * (zh * jax.nn.sigmoid(zh))).astype(o_ref.dtype)

    @pl.when(c == pl.num_programs(1) - 1)
    def _():
        s_out_ref[...] = state[...]


def gdn_mixer(gin, conv_buf8, s0, conv_w, a_log, dt_bias, gnorm, *, first_valid):
    B, L, _ = gin.shape
    assert L % CHUNK == 0
    bb = 2 if B % 2 == 0 else 1
    pad = jnp.zeros((LANES - 2 * GDN_HEADS,), F32)
    alog_row = jnp.concatenate([jnp.zeros((GDN_HEADS,), F32), a_log.astype(F32), pad])[None, :]
    dt_row = jnp.concatenate([jnp.zeros((GDN_HEADS,), F32), dt_bias.astype(F32), pad])[None, :]
    alog_col = alog_row[0, :8][:, None]
    dt_col = dt_row[0, :8][:, None]
    return pl.pallas_call(
        functools.partial(_gdn_kernel, first_valid=first_valid, bb=bb),
        grid=(B // bb, L // CHUNK),
        in_specs=[pl.BlockSpec((bb, CHUNK, GDN_IN_W), lambda b, c: (b, c, 0)),
                  pl.BlockSpec((bb, 8, GDN_CONV_CH), lambda b, c: (b, 0, 0)),
                  pl.BlockSpec((bb, GDN_HEADS, GDN_DK, GDN_DV), lambda b, c: (b, 0, 0, 0)),
                  pl.BlockSpec((CONV_W, GDN_CONV_CH), lambda b, c: (0, 0)),
                  pl.BlockSpec((1, LANES), lambda b, c: (0, 0)),
                  pl.BlockSpec((1, LANES), lambda b, c: (0, 0)),
                  pl.BlockSpec((8, 1), lambda b, c: (0, 0)),
                  pl.BlockSpec((8, 1), lambda b, c: (0, 0)),
                  pl.BlockSpec((1, GDN_DV), lambda b, c: (0, 0))],
        out_specs=[pl.BlockSpec((bb, CHUNK, GDN_V), lambda b, c: (b, c, 0)),
                   pl.BlockSpec((bb, GDN_HEADS, GDN_DK, GDN_DV), lambda b, c: (b, 0, 0, 0))],
        out_shape=[jax.ShapeDtypeStruct((B, L, GDN_V), BF16),
                   jax.ShapeDtypeStruct((B, GDN_HEADS, GDN_DK, GDN_DV), F32)],
        scratch_shapes=[pltpu.VMEM((bb, CHUNK + 8, GDN_CONV_CH), F32),
                        pltpu.VMEM((bb, GDN_HEADS, GDN_DK, GDN_DV), F32)],
        compiler_params=_cparams(("parallel", "arbitrary")), name="gdn_mixer")(
            gin, conv_buf8, s0, conv_w.astype(F32), alog_row, dt_row, alog_col, dt_col,
            gnorm.astype(F32)[None, :])


def _mix_ffn_kernel(h_ref, oa_ref, ob_ref, valid_ref, woa_ref, wob_ref, g_ref, w1_ref, w3_ref, w2_ref,
                    out_ref):
    h1 = (h_ref[...] + jnp.dot(oa_ref[...], woa_ref[...], preferred_element_type=F32)
          + jnp.dot(ob_ref[...], wob_ref[...], preferred_element_type=F32))
    u = _rms(h1, g_ref[...]).astype(BF16)
    out_ref[...] = h1
    for (a, b) in _col_chunks(0, D_FF):
        x1 = jnp.dot(u, w1_ref[:, a:b], preferred_element_type=F32)
        x3 = jnp.dot(u, w3_ref[:, a:b], preferred_element_type=F32)
        hid = (x1 * jax.nn.sigmoid(x1) * x3).astype(BF16)
        out_ref[...] += jnp.dot(hid, w2_ref[a:b, :], preferred_element_type=F32)
    out_ref[...] = jnp.where(valid_ref[...] > 0.0, out_ref[...], 0.0)


def mix_ffn(h, oa, ob, valid, woa, wob, g, w1, w3, w2, tm):
    M, D = h.shape
    assert M % tm == 0
    row = lambda i: (i, 0)
    return pl.pallas_call(
        _mix_ffn_kernel, grid=(M // tm,),
        in_specs=[pl.BlockSpec((tm, D), row), pl.BlockSpec((tm, SB_W), row), pl.BlockSpec((tm, GDN_V), row),
                  pl.BlockSpec((tm, 1), row), _const_spec(woa.shape), _const_spec(wob.shape),
                  _const_spec((1, D)), _const_spec(w1.shape), _const_spec(w3.shape), _const_spec(w2.shape)],
        out_specs=pl.BlockSpec((tm, D), row),
        out_shape=jax.ShapeDtypeStruct((M, D), F32),
        compiler_params=_cparams(("parallel",)), name="mix_ffn")(h, oa, ob, valid, woa, wob, g, w1, w3, w2)


def _swa_kernel(q_ref, k0_ref, k1_ref, k2_ref, v0_ref, v1_ref, v2_ref, sink_ref, o_ref, *,
                chunk_offset, k_first):
    c = pl.program_id(1) + chunk_offset
    C = CHUNK
    nk = (WIN_CHUNKS + 1) * C
    k = jnp.concatenate([k0_ref[0], k1_ref[0], k2_ref[0]], axis=0)
    v = jnp.concatenate([v0_ref[0], v1_ref[0], v2_ref[0]], axis=0)
    lane = lax.broadcasted_iota(jnp.int32, (nk, LANES), 1)
    lo = lane < SW_DIM
    k_sw = pltpu.roll(k, SW_DIM, 1)
    v_sw = pltpu.roll(v, SW_DIM, 1)
    key_pos = (c - WIN_CHUNKS) * C + lax.broadcasted_iota(jnp.int32, (1, nk), 1)
    key_ok = key_pos >= k_first
    pairs = SW_GROUP // 2
    rows = pairs * C
    vals = []
    scores = []
    for g in range(SW_KV_HEADS):
        ksrc_a, ksrc_b = (k, k_sw) if g == 0 else (k_sw, k)
        vsrc_a, vsrc_b = (v, v_sw) if g == 0 else (v_sw, v)
        k_a = jnp.where(lo, ksrc_a, 0.0).astype(BF16)
        k_b = jnp.where(lo, 0.0, ksrc_b).astype(BF16)
        vals += [jnp.where(lo, vsrc_a, 0.0).astype(BF16), jnp.where(lo, 0.0, vsrc_b).astype(BF16)]
        qs = jnp.concatenate([q_ref[0, :, (g * pairs + j) * LANES:(g * pairs + j + 1) * LANES]
                              for j in range(pairs)], axis=0)
        for kx in (k_a, k_b):
            scores.append(lax.dot_general(qs, kx, (((1,), (1,)), ((), ())), preferred_element_type=F32))
    s = jnp.where(key_ok, jnp.concatenate(scores, axis=0), NEG_BIG)
    snk = sink_ref[...]
    m = jnp.maximum(jnp.max(s, axis=-1, keepdims=True), snk)
    p = jnp.exp(s - m)
    den = jnp.sum(p, axis=-1, keepdims=True) + jnp.exp(snk - m)
    p = (p / den).astype(BF16)
    for g in range(SW_KV_HEADS):
        acc = (jnp.dot(p[(2 * g) * rows:(2 * g + 1) * rows], vals[2 * g], preferred_element_type=F32)
               + jnp.dot(p[(2 * g + 1) * rows:(2 * g + 2) * rows], vals[2 * g + 1], preferred_element_type=F32))
        for j in range(pairs):
            o_ref[0, :, (g * pairs + j) * LANES:(g * pairs + j + 1) * LANES] = (
                acc[j * C:(j + 1) * C, :].astype(o_ref.dtype))


def swa_attention(q, kv, sinks, *, n_chunks, chunk_offset, k_first):
    B = q.shape[0]
    pairs = SW_GROUP // 2
    sk = sinks.astype(F32).reshape(SW_KV_HEADS, pairs, 2).transpose(0, 2, 1)
    sk = jnp.repeat(sk.reshape(SW_KV_HEADS * 2 * pairs), CHUNK)[:, None]

    def kspec(d, blk):
        return pl.BlockSpec((1, CHUNK, LANES),
                            lambda b, c: (b, jnp.maximum(c + chunk_offset - d, 0), blk))

    return pl.pallas_call(
        functools.partial(_swa_kernel, chunk_offset=chunk_offset, k_first=k_first),
        grid=(B, n_chunks),
        in_specs=[pl.BlockSpec((1, CHUNK, SW_HEADS * SW_DIM), lambda b, c: (b, c, 0)),
                  kspec(2, 0), kspec(1, 0), kspec(0, 0), kspec(2, 1), kspec(1, 1), kspec(0, 1),
                  pl.BlockSpec((2 * SW_KV_HEADS * pairs * CHUNK, 1), lambda b, c: (0, 0))],
        out_specs=pl.BlockSpec((1, CHUNK, SW_HEADS * SW_DIM), lambda b, c: (b, c, 0)),
        out_shape=jax.ShapeDtypeStruct((B, n_chunks * CHUNK, SW_HEADS * SW_DIM), BF16),
        compiler_params=_cparams(("parallel", "parallel")), name="swa_attention")(
            q, kv, kv, kv, kv, kv, kv, sk)


def _mix_router_kernel(h_ref, o_ref, wo_ref, g_ref, rw_ref, rb_ref, h1_ref, u_ref, lg_ref):
    h1 = h_ref[...] + jnp.dot(o_ref[...], wo_ref[...], preferred_element_type=F32)
    h1_ref[...] = h1
    u = _rms(h1, g_ref[...])
    u_ref[...] = u.astype(u_ref.dtype)
    lg_ref[...] = _dot_hi(u, rw_ref[...]) + rb_ref[...]


def mix_router(h, o, wo, g, rw, rb, tm):
    M, D = h.shape
    assert M % tm == 0
    row = lambda i: (i, 0)
    return pl.pallas_call(
        _mix_router_kernel, grid=(M // tm,),
        in_specs=[pl.BlockSpec((tm, D), row), pl.BlockSpec((tm, D), row), _const_spec(wo.shape),
                  _const_spec((1, D)), _const_spec(rw.shape), _const_spec(rb.shape)],
        out_specs=[pl.BlockSpec((tm, D), row), pl.BlockSpec((tm, D), row), pl.BlockSpec((tm, LANES), row)],
        out_shape=[jax.ShapeDtypeStruct((M, D), F32), jax.ShapeDtypeStruct((M, D), BF16),
                   jax.ShapeDtypeStruct((M, LANES), F32)],
        compiler_params=_cparams(("parallel",)), name="mix_router")(h, o, wo, g, rw, rb)


def _expert_kernel(be_ref, nu_ref, x_ref, gate_ref, w1_ref, w3_ref, w2_ref, out_ref):
    i = pl.program_id(0)

    @pl.when(i < nu_ref[0])
    def _():
        x = x_ref[...]
        for (a, b) in _col_chunks(0, EXPERT_FF):
            x1 = jnp.dot(x, w1_ref[0, :, a:b], preferred_element_type=F32)
            x3 = jnp.dot(x, w3_ref[0, :, a:b], preferred_element_type=F32)
            hid = (x1 * jax.nn.sigmoid(x1) * x3).astype(BF16)
            y = jnp.dot(hid, w2_ref[0, a:b, :], preferred_element_type=F32)
            if a == 0:
                out_ref[...] = y
            else:
                out_ref[...] += y
        out_ref[...] = out_ref[...] * gate_ref[...]

    @pl.when(i >= nu_ref[0])
    def _():
        out_ref[...] = jnp.zeros(out_ref.shape, F32)


def expert_ffn(blk_expert, n_used, xs, gates, w1, w3, w2, blk):
    n_slot, D = xs.shape
    n_blk = n_slot // blk
    wspec = lambda shape: pl.BlockSpec((1,) + shape, lambda i, be, nu: (be[i], 0, 0),
                                       pipeline_mode=pl.Buffered(1))
    return pl.pallas_call(
        _expert_kernel,
        grid_spec=pltpu.PrefetchScalarGridSpec(
            num_scalar_prefetch=2, grid=(n_blk,),
            in_specs=[pl.BlockSpec((blk, D), lambda i, be, nu: (i, 0)),
                      pl.BlockSpec((blk, 1), lambda i, be, nu: (i, 0)),
                      wspec((D, EXPERT_FF)), wspec((D, EXPERT_FF)), wspec((EXPERT_FF, D))],
            out_specs=pl.BlockSpec((blk, D), lambda i, be, nu: (i, 0))),
        out_shape=jax.ShapeDtypeStruct((n_slot, D), F32),
        compiler_params=_cparams(("arbitrary",)), name="expert_ffn")(
            blk_expert, n_used, xs, gates, w1, w3, w2)


def _final_kernel(h_ref, y_ref, g_ref, out_ref):
    out_ref[0] = _rms(h_ref[0] + y_ref[0], g_ref[...])


def final_norm(h, y, g, *, skip_rows, tm):
    B, L, D = h.shape
    assert skip_rows % tm == 0 and (L - skip_rows) % tm == 0
    off = skip_rows // tm
    return pl.pallas_call(
        _final_kernel, grid=(B, (L - skip_rows) // tm),
        in_specs=[pl.BlockSpec((1, tm, D), lambda b, i: (b, i + off, 0)),
                  pl.BlockSpec((1, tm, D), lambda b, i: (b, i + off, 0)),
                  pl.BlockSpec((1, D), lambda b, i: (0, 0))],
        out_specs=pl.BlockSpec((1, tm, D), lambda b, i: (b, i, 0)),
        out_shape=jax.ShapeDtypeStruct((B, L - skip_rows, D), F32),
        compiler_params=_cparams(("parallel", "parallel")), name="final_norm")(h, y, g)


def _route(logits, blk):
    n_tok = logits.shape[0]
    n_asg = n_tok * TOP_K
    top_val, top_idx = lax.top_k(logits, TOP_K)
    gates = jax.nn.softmax(top_val, axis=-1).reshape(-1)
    expert = top_idx.reshape(-1).astype(jnp.int32)
    onehot = (expert[:, None] == jnp.arange(N_EXPERTS, dtype=jnp.int32)[None, :]).astype(jnp.int32)
    csum = jnp.cumsum(onehot, axis=0)
    counts = csum[-1]
    rank = jnp.take_along_axis(csum, expert[:, None], axis=1)[:, 0] - 1
    padded = (counts + blk - 1) // blk * blk
    pend = jnp.cumsum(padded)
    pstart = pend - padded
    dest = pstart[expert] + rank
    n_blk = (n_asg + N_EXPERTS * (blk - 1) + blk - 1) // blk
    n_slot = n_blk * blk
    token = jnp.arange(n_asg, dtype=jnp.int32) // TOP_K
    slot_tok = jnp.full((n_slot,), n_tok, jnp.int32).at[dest].set(token)
    slot_gate = jnp.zeros((n_slot,), F32).at[dest].set(gates)
    blk_expert = jnp.minimum(jnp.searchsorted(pend, jnp.arange(n_blk, dtype=jnp.int32) * blk, side='right'),
                             N_EXPERTS - 1).astype(jnp.int32)
    n_used = (pend[-1] // blk).astype(jnp.int32).reshape(1)
    return slot_tok, slot_gate, blk_expert, n_used, dest.reshape(n_tok, TOP_K)


def _moe(u_bf, logits, w1, w3, w2, blk):
    n_tok = u_bf.shape[0]
    slot_tok, slot_gate, blk_expert, n_used, dest = _route(logits[:, :N_EXPERTS], blk)
    xs_pad = jnp.concatenate([u_bf, jnp.zeros((1, D_MODEL), u_bf.dtype)], axis=0)
    xs = xs_pad[slot_tok]
    yb = expert_ffn(blk_expert, n_used, xs, slot_gate[:, None], w1, w3, w2, blk)
    return yb[dest[:, 0]] + yb[dest[:, 1]]


def _rope_tables(pos):
    half = SW_DIM // 2
    inv_freq = ROPE_THETA ** (-jnp.arange(half, dtype=F32) / half)
    ang = pos.astype(F32)[:, None] * inv_freq[None, :]
    cos, sin = jnp.cos(ang), jnp.sin(ang)
    cos = jnp.concatenate([cos, cos, cos, cos], axis=1)
    sin = jnp.concatenate([-sin, sin, -sin, sin], axis=1)
    return cos, sin


def _row_tile(m, pref):
    t = pref
    while m % t:
        t //= 2
    return t


def kernel(x_prompt, x_sample, cache_sb_k, cache_sb_v, state_gdn, state_gdn_conv, cache_sw_k, cache_sw_v, meta_tokens, norm_mix, norm_ffn, norm_final, w_in_even, gdn_conv_w, gdn_a_log, gdn_dt_bias, gdn_norm, w_out_even, ffn_w1, ffn_w3, ffn_w2, w_in_odd, sw_sinks, w_out_odd, router_w, router_b, moe_w1, moe_w3, moe_w2):
    nb, seq, D = x_prompt.shape
    ns, tn, _ = x_sample.shape
    past = cache_sb_k.shape[1]
    lp = PREFIX + seq

    c_q, c_k, c_v, c_x, c_b, c_a, c_z = np.cumsum((0, SB_W, SB_W, SB_W, GDN_CONV_CH, GDN_HEADS, GDN_HEADS)).tolist()
    w_e = w_in_even.astype(F32)
    w_even = jnp.concatenate([
        w_e[:, c_q:c_k] * (SB_DIM ** -0.5), w_e[:, c_k:c_x], w_e[:, c_x:c_b], w_e[:, c_z:c_z + GDN_V],
        w_e[:, c_b:c_z], jnp.zeros((D, LANES - 2 * GDN_HEADS), F32)], axis=1).astype(BF16)
    qkv_w = 3 * SB_W
    even_groups = ((0, SB_W, (BF16,)), (SB_W, qkv_w, (BF16, F32)), (qkv_w, qkv_w + GDN_IN_W, (F32,)))
    w_o = w_in_odd.astype(F32)
    q_w = SW_HEADS * SW_DIM
    kv_w = SW_KV_HEADS * SW_DIM
    w_odd = jnp.concatenate([w_o[:, :q_w] * (SW_DIM ** -0.5), w_o[:, q_w:]], axis=1).astype(BF16)
    odd_groups = ((0, q_w, (BF16,)), (q_w, q_w + 2 * kv_w, (F32,)))
    woa = w_out_even[:SB_W].astype(BF16)
    wob = w_out_even[SB_W:].astype(BF16)
    w1, w3, w2 = ffn_w1.astype(BF16), ffn_w3.astype(BF16), ffn_w2.astype(BF16)
    wo_odd = w_out_odd.astype(BF16)
    rw = jnp.concatenate([router_w.astype(F32), jnp.zeros((D, LANES - N_EXPERTS), F32)], axis=1)
    rb = jnp.concatenate([router_b.astype(F32), jnp.zeros((LANES - N_EXPERTS,), F32)])[None, :]
    mw1, mw3, mw2 = moe_w1.astype(BF16), moe_w3.astype(BF16), moe_w2.astype(BF16)
    g_mix = norm_mix.astype(F32)
    g_ffn = norm_ffn.astype(F32)
    g_fin = norm_final.astype(F32)[None, :]

    def layers(h, B, L, first_valid, sb_fn, conv_buf, s0, pos, sw_fn, moe_blk):
        M = B * L
        tm = _row_tile(M, 512)
        valid = (jnp.arange(L) >= first_valid).astype(F32)
        valid = jnp.broadcast_to(valid[None, :], (B, L)).reshape(M, 1)
        q_bf, kv_bf, kv_f32, gin = norm_proj(h, g_mix[0:1], w_even, even_groups, tm)
        oa = sb_fn(q_bf.reshape(B, L, SB_W), kv_bf.reshape(B, L, 2 * SB_W))
        conv8 = jnp.concatenate([jnp.zeros((B, 8 - (CONV_W - 1), GDN_CONV_CH), F32), conv_buf.astype(F32)], axis=1)
        gin3 = gin.reshape(B, L, GDN_IN_W)
        ob, s_new = gdn_mixer(gin3, conv8, s0.astype(F32), gdn_conv_w, gdn_a_log, gdn_dt_bias, gdn_norm,
                              first_valid=first_valid)
        conv_new = jnp.concatenate([conv_buf.astype(F32), gin3[:, :, :GDN_CONV_CH]], axis=1)[:, -(CONV_W - 1):]
        h = mix_ffn(h, oa.reshape(M, SB_W), ob.reshape(M, GDN_V), valid, woa, wob, g_ffn[0:1], w1, w3, w2, tm)
        cos, sin = _rope_tables(jnp.broadcast_to(pos[None, :], (B, L)).reshape(M))
        q_sw, kv_sw = norm_proj(h, g_mix[1:2], w_odd, odd_groups, tm, rope=(cos, sin, q_w + kv_w))
        kv_sw = kv_sw.reshape(B, L, 2 * kv_w)
        o_sw = sw_fn(q_sw.reshape(B, L, q_w), kv_sw)
        h1, u_bf, logits = mix_router(h, o_sw.reshape(M, q_w), wo_odd, g_ffn[1:2], rw, rb, tm)
        y = _moe(u_bf, logits, mw1, mw3, mw2, moe_blk)
        kv4 = kv_f32.reshape(B, L, 2, SB_HEADS, SB_DIM)
        return h1.reshape(B, L, D), y.reshape(B, L, D), kv4[:, :, 0], kv4[:, :, 1], s_new, conv_new, kv_sw

    dt = x_prompt.dtype
    hp = jnp.concatenate([jnp.zeros((nb, FIRST_VALID, D), dt),
                          jnp.broadcast_to(meta_tokens.astype(dt), (nb, N_META_TOK, D)), x_prompt], axis=1)
    pos_p = jnp.arange(lp, dtype=jnp.int32) - FIRST_VALID

    def sb_prompt(q, kv):
        return sb_attention(q, kv, kv, 0, SB_W // (SB_HB * LANES), tq=128, tk=256, q_start=0,
                            k_first=FIRST_VALID)

    def sw_prompt(q, kv):
        return swa_attention(q, kv, sw_sinks, n_chunks=lp // CHUNK, chunk_offset=0, k_first=FIRST_VALID)

    h1p, yp, sbk_p, sbv_p, gdn_p, conv_p, kvsw_p = layers(
        hp.reshape(nb * lp, D), nb, lp, FIRST_VALID, sb_prompt,
        jnp.zeros((nb, CONV_W - 1, GDN_CONV_CH), dt), jnp.zeros((nb, GDN_HEADS, GDN_DK, GDN_DV), dt),
        pos_p, sw_prompt, 512)
    y_prompt = final_norm(h1p, yp, g_fin, skip_rows=PREFIX, tm=128)

    pos_s = N_META_TOK + past + jnp.arange(tn, dtype=jnp.int32)

    def sb_samp(q, kv):
        kk = jnp.concatenate([cache_sb_k.reshape(ns, past, SB_W).astype(BF16), kv[:, :, :SB_W]], axis=1)
        vv = jnp.concatenate([cache_sb_v.reshape(ns, past, SB_W).astype(BF16), kv[:, :, SB_W:]], axis=1)
        return sb_attention(q, kk, vv, 0, 0, tq=tn, tk=64, q_start=past, k_first=0)

    def sw_samp(q, kv):
        kc = cache_sw_k.reshape(ns, -1, kv_w).astype(F32)
        vc = cache_sw_v.reshape(ns, -1, kv_w).astype(F32)
        kv_all = jnp.concatenate([jnp.concatenate([kc, vc], axis=2), kv], axis=1)
        return swa_attention(q, kv_all, sw_sinks, n_chunks=1, chunk_offset=WIN_CHUNKS, k_first=0)

    h1s, ys, sbk_s, sbv_s, gdn_s, conv_s, kvsw_s = layers(
        x_sample.reshape(ns * tn, D), ns, tn, 0, sb_samp, state_gdn_conv, state_gdn, pos_s, sw_samp, 128)
    y_sample = final_norm(h1s, ys, g_fin, skip_rows=0, tm=tn)

    def sw_split(kv):
        return (kv[:, :, :kv_w].reshape(kv.shape[0], -1, SW_KV_HEADS, SW_DIM),
                kv[:, :, kv_w:].reshape(kv.shape[0], -1, SW_KV_HEADS, SW_DIM))

    swk_p, swv_p = sw_split(kvsw_p[:, -WINDOW:])
    swk_s, swv_s = sw_split(kvsw_s)
    buf = cache_sw_k.shape[1]
    sw_k_s = jnp.concatenate([cache_sw_k, swk_s.astype(cache_sw_k.dtype)], axis=1)[:, -buf:]
    sw_v_s = jnp.concatenate([cache_sw_v, swv_s.astype(cache_sw_v.dtype)], axis=1)[:, -buf:]
    return (y_prompt, y_sample,
            sbk_p[:, FIRST_VALID:], sbv_p[:, FIRST_VALID:], sbk_s, sbv_s,
            gdn_p.astype(dt), gdn_s.astype(state_gdn.dtype), conv_p.astype(dt), conv_s.astype(dt),
            swk_p, swv_p, sw_k_s, sw_v_s)
```

```python
import functools
import math

import numpy as np
import jax
import jax.numpy as jnp
from jax import lax
from jax.experimental import pallas as pl
from jax.experimental.pallas import tpu as pltpu

F32 = jnp.float32
BF16 = jnp.bfloat16
HI = lax.Precision.HIGHEST

D_MODEL = 1024
CHUNK = 64
N_META_TOK = 16
PREFIX = 128
FIRST_VALID = PREFIX - N_META_TOK
RMS_EPS = 1e-6
L2_EPS = 1e-6

SB_HEADS = 8
SB_DIM = 64
SB_W = SB_HEADS * SB_DIM
GDN_HEADS = 4
GDN_DK = 128
GDN_DV = 128
CONV_W = 4
GDN_QK = GDN_HEADS * GDN_DK
GDN_V = GDN_HEADS * GDN_DV
GDN_CONV_CH = 2 * GDN_QK + GDN_V
GDN_IN_W = GDN_CONV_CH + GDN_V + 128
SW_HEADS = 16
SW_KV_HEADS = 2
SW_GROUP = SW_HEADS // SW_KV_HEADS
SW_DIM = 64
WINDOW = 128
WIN_CHUNKS = WINDOW // CHUNK
ROPE_THETA = 10000.0
D_FF = 2816
N_EXPERTS = 8
TOP_K = 2
EXPERT_FF = 3584

LANES = 128
VMEM_LIMIT = 56 * 1024 * 1024
NEG_BIG = -1e30
SB_EXIT = 120.0


def _cparams(sem):
    return pltpu.CompilerParams(dimension_semantics=sem, vmem_limit_bytes=VMEM_LIMIT)


def _const_spec(shape):
    nd = len(shape)
    return pl.BlockSpec(shape, lambda *_: (0,) * nd, pipeline_mode=pl.Buffered(1))


def _rms(x, g):
    ms = jnp.mean(x * x, axis=-1, keepdims=True)
    return x * lax.rsqrt(ms + RMS_EPS) * g


def _col_chunks(c0, c1, step=512):
    out = []
    while c0 < c1:
        out.append((c0, min(c0 + step, c1)))
        c0 = out[-1][1]
    return out


def _norm_proj_kernel(*refs, groups, rope_cols):
    if rope_cols:
        x_ref, g_ref, w_ref, cos_ref, sin_ref = refs[:5]
        outs = refs[5:]
    else:
        x_ref, g_ref, w_ref = refs[:3]
        outs = refs[3:]
    u = _rms(x_ref[...], g_ref[...]).astype(BF16)
    if rope_cols:
        cos = cos_ref[...]
        sin = sin_ref[...]
        lane = lax.broadcasted_iota(jnp.int32, cos.shape, 1)
        first_half = (lane % SW_DIM) < (SW_DIM // 2)
    oi = 0
    for (c0, c1, dtypes) in groups:
        for (a, b) in _col_chunks(c0, c1):
            y = jnp.dot(u, w_ref[:, a:b], preferred_element_type=F32)
            if a < rope_cols:
                pieces = []
                for p in range(0, b - a, LANES):
                    yp = y[:, p:p + LANES]
                    if a + p < rope_cols:
                        rot = jnp.where(first_half, pltpu.roll(yp, LANES - SW_DIM // 2, 1),
                                        pltpu.roll(yp, SW_DIM // 2, 1))
                        yp = yp * cos + rot * sin
                    pieces.append(yp)
                y = jnp.concatenate(pieces, axis=1) if len(pieces) > 1 else pieces[0]
            for k, dt in enumerate(dtypes):
                outs[oi + k][:, a - c0:b - c0] = y.astype(dt)
        oi += len(dtypes)


def norm_proj(x, g, w, groups, tm, rope=None):
    M, D = x.shape
    N = w.shape[1]
    assert M % tm == 0
    rope_cols = rope[2] if rope is not None else 0
    in_specs = [pl.BlockSpec((tm, D), lambda i: (i, 0)), _const_spec((1, D)), _const_spec((D, N))]
    args = [x, g, w]
    if rope is not None:
        reps = rope[0].shape[0] // tm
        assert rope[0].shape[0] == reps * tm and M % rope[0].shape[0] == 0
        in_specs += [pl.BlockSpec((tm, LANES), lambda i: (i % reps, 0))] * 2
        args += [rope[0], rope[1]]
    out_shape, out_specs = [], []
    for (c0, c1, dtypes) in groups:
        for dt in dtypes:
            out_shape.append(jax.ShapeDtypeStruct((M, c1 - c0), dt))
            out_specs.append(pl.BlockSpec((tm, c1 - c0), lambda i: (i, 0)))
    return pl.pallas_call(
        functools.partial(_norm_proj_kernel, groups=groups, rope_cols=rope_cols),
        grid=(M // tm,), in_specs=in_specs, out_specs=out_specs, out_shape=out_shape,
        compiler_params=_cparams(("parallel",)), name="norm_proj")(*args)


SB_HB = 2
SB_SUB = 256


def _sb_kernel(q_ref, k_ref, v_ref, o_ref, run_ref, acc_ref, *, tq, tk, q_start, k_first):
    i = pl.program_id(2)
    q0 = q_start + i * tq
    align = math.gcd(math.gcd(tq, tk), q_start) if q_start else math.gcd(tq, tk)
    lane_q = lax.broadcasted_iota(jnp.int32, (tq, LANES), 1)
    nh = 2 * SB_HB
    q_heads = []
    for lb in range(SB_HB):
        q = q_ref[0, :, lb * LANES:(lb + 1) * LANES]
        zero_q = jnp.zeros_like(q)
        q_heads += [jnp.where(lane_q < SB_DIM, q, zero_q), jnp.where(lane_q >= SB_DIM, q, zero_q)]
    qh = jnp.stack(q_heads)
    sub = min(tk, SB_SUB)
    n_sub = tk // sub
    r = lax.broadcasted_iota(jnp.int32, (sub, sub), 0)
    c = lax.broadcasted_iota(jnp.int32, (sub, sub), 1)
    later_mat = jnp.where(r > c, 1.0, 0.0).astype(BF16)
    run_ref[...] = jnp.zeros(run_ref.shape, F32)
    acc_ref[...] = jnp.zeros(acc_ref.shape, F32)

    def cond(carry):
        _, hi, alive = carry
        return jnp.logical_and(hi > 0, alive > 0)

    def body(carry):
        s, hi, _ = carry
        s = pl.multiple_of(s, align)
        qpos = q0 + lax.broadcasted_iota(jnp.int32, (tq, tk), 0)
        kpos = s + lax.broadcasted_iota(jnp.int32, (tq, tk), 1)
        lim = jnp.minimum(qpos, hi)
        if k_first:
            vis = (pltpu.bitcast(kpos - k_first, jnp.uint32)
                   < pltpu.bitcast(jnp.maximum(lim - k_first, 0), jnp.uint32))
        else:
            vis = kpos < lim
        vis = vis[None]
        kts = [k_ref[0, pl.ds(s, tk), lb * LANES:(lb + 1) * LANES] for lb in range(SB_HB)]
        vts = [v_ref[0, pl.ds(s, tk), lb * LANES:(lb + 1) * LANES] for lb in range(SB_HB)]
        kt = jnp.stack([kts[hh // 2] for hh in range(nh)])
        vt = jnp.stack([vts[hh // 2] for hh in range(nh)])
        z = lax.dot_general(qh, kt, (((2,), (2,)), ((0,), (0,))), preferred_element_type=F32)
        lk0 = -(jnp.maximum(z, 0.0) + jnp.log(1.0 + jnp.exp(-jnp.abs(z))))
        lk = jnp.where(vis, lk0, 0.0)
        lk_bf = lk.astype(BF16).reshape(nh * tq, tk)
        carry_sum = run_ref[...]
        parts = [None] * n_sub
        for b in range(n_sub - 1, -1, -1):
            cs = jnp.dot(lk_bf[:, b * sub:(b + 1) * sub], later_mat, preferred_element_type=F32)
            parts[b] = cs.reshape(nh, tq, sub) + carry_sum
            if b:
                carry_sum = carry_sum + jnp.sum(lk[:, :, b * sub:(b + 1) * sub], axis=-1, keepdims=True)
        later = jnp.concatenate(parts, axis=2) if n_sub > 1 else parts[0]
        w = jnp.where(vis, jnp.exp(z + lk0 + later), 0.0)
        acc_ref[...] += lax.dot_general(w.astype(BF16), vt, (((2,), (1,)), ((0,), (0,))),
                                        preferred_element_type=F32)
        run = run_ref[...] + jnp.sum(lk, axis=-1, keepdims=True)
        run_ref[...] = run
        alive = (jnp.max(run) > -SB_EXIT).astype(jnp.int32)
        return (jnp.maximum(s - tk, 0), s, alive)

    lax.while_loop(cond, body, (jnp.maximum(q0 + tq - tk, 0), q0 + tq, jnp.int32(1)))
    for lb in range(SB_HB):
        o_ref[0, :, lb * LANES:(lb + 1) * LANES] = jnp.where(
            lane_q < SB_DIM, acc_ref[2 * lb], acc_ref[2 * lb + 1]).astype(o_ref.dtype)


def sb_attention(q, k, v, k_off, v_off, *, tq, tk, q_start, k_first):
    B, Lq, _ = q.shape
    Lk = k.shape[1]
    wb = SB_HB * LANES
    assert Lq % tq == 0 and Lk >= tk and (q_start + tq - tk) % 16 == 0 and tk % min(tk, SB_SUB) == 0
    return pl.pallas_call(
        functools.partial(_sb_kernel, tq=tq, tk=tk, q_start=q_start, k_first=k_first),
        grid=(B, SB_W // wb, Lq // tq),
        in_specs=[pl.BlockSpec((1, tq, wb), lambda b, p, i: (b, i, p)),
                  pl.BlockSpec((1, Lk, wb), lambda b, p, i: (b, 0, k_off + p)),
                  pl.BlockSpec((1, Lk, wb), lambda b, p, i: (b, 0, v_off + p))],
        out_specs=pl.BlockSpec((1, tq, wb), lambda b, p, i: (b, i, p)),
        out_shape=jax.ShapeDtypeStruct((B, Lq, SB_W), BF16),
        scratch_shapes=[pltpu.VMEM((2 * SB_HB, tq, 1), F32), pltpu.VMEM((2 * SB_HB, tq, LANES), F32)],
        compiler_params=_cparams(("parallel", "parallel", "arbitrary")), name="sb_attention")(q, k, v)


def _dot_hi(a, b):
    return jnp.dot(a, b, preferred_element_type=F32, precision=HI)


def _dot_bf(a, b):
    return jnp.dot(a.astype(BF16), b.astype(BF16), preferred_element_type=F32)


def _split_bf(a, parts):
    out = []
    for _ in range(parts - 1):
        hi = a.astype(BF16)
        out.append(hi)
        a = a - hi.astype(F32)
    out.append(a.astype(BF16))
    return out


def _dot_split(a_parts, b_parts):
    acc = None
    for i, ap in enumerate(a_parts):
        for j, bp in enumerate(b_parts):
            if i + j < max(len(a_parts), len(b_parts)):
                d = jnp.dot(ap, bp, preferred_element_type=F32)
                acc = d if acc is None else acc + d
    return acc


def _bdot(a, b, dims):
    return lax.dot_general(a, b, (dims, ((0,), (0,))), preferred_element_type=F32)


def _bdot_split(a_parts, b_parts):
    acc = None
    for i, ap in enumerate(a_parts):
        for j, bp in enumerate(b_parts):
            if i + j < max(len(a_parts), len(b_parts)):
                d = _bdot(ap, bp, ((2,), (1,)))
                acc = d if acc is None else acc + d
    return acc


def _gdn_kernel(gin_ref, cbuf_ref, s0_ref, cw_ref, alog_ref, dt_ref, alogc_ref, dtc_ref, gn_ref,
                o_ref, s_out_ref, xbuf, state, *, first_valid, bb):
    c = pl.program_id(1)
    C = CHUNK
    H = GDN_HEADS

    @pl.when(c == 0)
    def _():
        for bi in range(bb):
            state[bi * H:(bi + 1) * H] = s0_ref[bi]
        xbuf[:, 0:8, :] = cbuf_ref[...]

    ri = lax.broadcasted_iota(jnp.int32, (C, C), 0)
    ci = lax.broadcasted_iota(jnp.int32, (C, C), 1)
    incl = (ri >= ci)[None]
    strict = (ri > ci)[None]
    tril = jnp.where(ri >= ci, 1.0, 0.0).astype(BF16)
    triu = jnp.where(ri <= ci, 1.0, 0.0).astype(BF16)
    row_t = c * C + lax.broadcasted_iota(jnp.int32, (C, LANES), 0)
    col_t = c * C + lax.broadcasted_iota(jnp.int32, (8, C), 1)

    qs, ks, vs, zs, gcs, grs, bcs = [], [], [], [], [], [], []
    for bi in range(bb):
        xbuf[bi, 8:8 + C, :] = gin_ref[bi, :, 0:GDN_CONV_CH]
        y = xbuf[bi, 8:8 + C, :] * cw_ref[CONV_W - 1:CONV_W, :]
        for d in range(1, CONV_W):
            y = y + xbuf[bi, 8 - d:8 - d + C, :] * cw_ref[CONV_W - 1 - d:CONV_W - d, :]
        xbuf[bi, 0:8, :] = xbuf[bi, C:C + 8, :]
        conv = y * jax.nn.sigmoid(y)
        ba = gin_ref[bi, :, GDN_CONV_CH + GDN_V:GDN_IN_W]
        beta_col = jnp.where(row_t >= first_valid, jax.nn.sigmoid(ba), 0.0)
        g_col = jnp.where(row_t >= first_valid,
                          -jnp.exp(alog_ref[...]) * jax.nn.softplus(ba + dt_ref[...]), 0.0)
        ba_t = ba.T
        g_row = jnp.where(col_t >= first_valid,
                          -jnp.exp(alogc_ref[...]) * jax.nn.softplus(ba_t[0:8, :] + dtc_ref[...]), 0.0)
        G_col = _dot_split([tril], _split_bf(g_col, 3))
        G_row = _dot_split(_split_bf(g_row, 3), [triu])
        for h in range(H):
            qs.append(conv[:, h * GDN_DK:(h + 1) * GDN_DK])
            ks.append(conv[:, GDN_QK + h * GDN_DK:GDN_QK + (h + 1) * GDN_DK])
            vs.append(conv[:, 2 * GDN_QK + h * GDN_DV:2 * GDN_QK + (h + 1) * GDN_DV])
            zs.append(gin_ref[bi, :, GDN_CONV_CH + h * GDN_DV:GDN_CONV_CH + (h + 1) * GDN_DV])
            gcs.append(G_col[:, H + h:H + h + 1])
            grs.append(G_row[H + h:H + h + 1, :])
            bcs.append(beta_col[:, h:h + 1])
    q = jnp.stack(qs)
    k = jnp.stack(ks)
    v = jnp.stack(vs)
    z = jnp.stack(zs)
    gc = jnp.stack(gcs)
    gr = jnp.stack(grs)
    bc = jnp.stack(bcs)
    q = q * lax.rsqrt(jnp.sum(q * q, axis=-1, keepdims=True) + L2_EPS) * (GDN_DK ** -0.5)
    k = k * lax.rsqrt(jnp.sum(k * k, axis=-1, keepdims=True) + L2_EPS)
    g_last = gc[:, C - 1:C, :]
    decay = jnp.where(incl, jnp.exp(jnp.minimum(gc - gr, 0.0)), 0.0)
    k_bf = k.astype(BF16)
    kk = _bdot(k_bf, k_bf, ((2,), (2,)))
    qk = _bdot(q.astype(BF16), k_bf, ((2,), (2,))) * decay
    pw = -jnp.where(strict, bc * kk * decay, 0.0)
    sol = jnp.concatenate([bc * v, (bc * jnp.exp(gc)) * k], axis=2)
    steps = int(math.log2(C))
    for s in range(steps):
        pw_parts = _split_bf(pw, 2)
        sol = sol + _bdot_split(pw_parts, _split_bf(sol, 2))
        if s + 1 < steps:
            pw = _bdot_split(pw_parts, pw_parts)
    u_v = sol[:, :, :GDN_DV]
    w_k = sol[:, :, GDN_DV:]
    S = state[...]
    S_bf = S.astype(BF16)
    u = u_v - _bdot(w_k.astype(BF16), S_bf, ((2,), (1,)))
    o = (_bdot((q * jnp.exp(gc)).astype(BF16), S_bf, ((2,), (1,)))
         + _bdot(qk.astype(BF16), u.astype(BF16), ((2,), (1,))))
    k_tail = k * jnp.exp(g_last - gc)
    k_tail_t = jnp.stack([k_tail[n].T for n in range(bb * H)]).astype(BF16)
    state[...] = jnp.exp(g_last) * S + _bdot(k_tail_t, u.astype(BF16), ((2,), (1,)))
    o = o * lax.rsqrt(jnp.mean(o * o, axis=-1, keepdims=True) + RMS_EPS) * gn_ref[...]
    o = (o * (z * jax.nn.sigmoid(z))).astype(o_ref.dtype)
    for bi in range(bb):
        for h in range(H):
            o_ref[bi, :, h * GDN_DV:(h + 1) * GDN_DV] = o[bi * H + h]

    @pl.when(c == pl.num_programs(1) - 1)
    def _():
        for bi in range(bb):
            s_out_ref[bi] = state[bi * H:(bi + 1) * H]


def gdn_mixer(gin, conv_buf8, s0, conv_w, a_log, dt_bias, gnorm, *, first_valid):
    B, L, _ = gin.shape
    assert L % CHUNK == 0
    bb = max(d for d in (8, 4, 2, 1) if B % d == 0)
    pad = jnp.zeros((LANES - 2 * GDN_HEADS,), F32)
    alog_row = jnp.concatenate([jnp.zeros((GDN_HEADS,), F32), a_log.astype(F32), pad])[None, :]
    dt_row = jnp.concatenate([jnp.zeros((GDN_HEADS,), F32), dt_bias.astype(F32), pad])[None, :]
    alog_col = alog_row[0, :8][:, None]
    dt_col = dt_row[0, :8][:, None]
    return pl.pallas_call(
        functools.partial(_gdn_kernel, first_valid=first_valid, bb=bb),
        grid=(B // bb, L // CHUNK),
        in_specs=[pl.BlockSpec((bb, CHUNK, GDN_IN_W), lambda b, c: (b, c, 0)),
                  pl.BlockSpec((bb, 8, GDN_CONV_CH), lambda b, c: (b, 0, 0)),
                  pl.BlockSpec((bb, GDN_HEADS, GDN_DK, GDN_DV), lambda b, c: (b, 0, 0, 0)),
                  pl.BlockSpec((CONV_W, GDN_CONV_CH), lambda b, c: (0, 0)),
                  pl.BlockSpec((1, LANES), lambda b, c: (0, 0)),
                  pl.BlockSpec((1, LANES), lambda b, c: (0, 0)),
                  pl.BlockSpec((8, 1), lambda b, c: (0, 0)),
                  pl.BlockSpec((8, 1), lambda b, c: (0, 0)),
                  pl.BlockSpec((1, GDN_DV), lambda b, c: (0, 0))],
        out_specs=[pl.BlockSpec((bb, CHUNK, GDN_V), lambda b, c: (b, c, 0)),
                   pl.BlockSpec((bb, GDN_HEADS, GDN_DK, GDN_DV), lambda b, c: (b, 0, 0, 0))],
        out_shape=[jax.ShapeDtypeStruct((B, L, GDN_V), BF16),
                   jax.ShapeDtypeStruct((B, GDN_HEADS, GDN_DK, GDN_DV), F32)],
        scratch_shapes=[pltpu.VMEM((bb, CHUNK + 8, GDN_CONV_CH), F32),
                        pltpu.VMEM((bb * GDN_HEADS, GDN_DK, GDN_DV), F32)],
        compiler_params=_cparams(("parallel", "arbitrary")), name="gdn_mixer")(
            gin, conv_buf8, s0, conv_w.astype(F32), alog_row, dt_row, alog_col, dt_col,
            gnorm.astype(F32)[None, :])


def _mix_ffn_kernel(h_ref, oa_ref, ob_ref, valid_ref, woa_ref, wob_ref, g_ref, w1_ref, w3_ref, w2_ref,
                    out_ref):
    h1 = (h_ref[...] + jnp.dot(oa_ref[...], woa_ref[...], preferred_element_type=F32)
          + jnp.dot(ob_ref[...], wob_ref[...], preferred_element_type=F32))
    u = _rms(h1, g_ref[...]).astype(BF16)
    out_ref[...] = h1
    for (a, b) in _col_chunks(0, D_FF):
        x1 = jnp.dot(u, w1_ref[:, a:b], preferred_element_type=F32)
        x3 = jnp.dot(u, w3_ref[:, a:b], preferred_element_type=F32)
        hid = (x1 * jax.nn.sigmoid(x1) * x3).astype(BF16)
        out_ref[...] += jnp.dot(hid, w2_ref[a:b, :], preferred_element_type=F32)
    out_ref[...] = jnp.where(valid_ref[...] > 0.0, out_ref[...], 0.0)


def mix_ffn(h, oa, ob, valid, woa, wob, g, w1, w3, w2, tm):
    M, D = h.shape
    reps = valid.shape[0] // tm
    assert M % tm == 0 and valid.shape[0] == reps * tm and M % valid.shape[0] == 0
    row = lambda i: (i, 0)
    return pl.pallas_call(
        _mix_ffn_kernel, grid=(M // tm,),
        in_specs=[pl.BlockSpec((tm, D), row), pl.BlockSpec((tm, SB_W), row), pl.BlockSpec((tm, GDN_V), row),
                  pl.BlockSpec((tm, 1), lambda i: (i % reps, 0)), _const_spec(woa.shape), _const_spec(wob.shape),
                  _const_spec((1, D)), _const_spec(w1.shape), _const_spec(w3.shape), _const_spec(w2.shape)],
        out_specs=pl.BlockSpec((tm, D), row),
        out_shape=jax.ShapeDtypeStruct((M, D), F32),
        compiler_params=_cparams(("parallel",)), name="mix_ffn")(h, oa, ob, valid, woa, wob, g, w1, w3, w2)


def _swa_kernel(q_ref, k0_ref, k1_ref, k2_ref, v0_ref, v1_ref, v2_ref, sink_ref, o_ref, *,
                chunk_offset, k_first):
    c = pl.program_id(1) + chunk_offset
    C = CHUNK
    nk = (WIN_CHUNKS + 1) * C
    k = jnp.concatenate([k0_ref[0], k1_ref[0], k2_ref[0]], axis=0)
    v = jnp.concatenate([v0_ref[0], v1_ref[0], v2_ref[0]], axis=0)
    lane = lax.broadcasted_iota(jnp.int32, (nk, LANES), 1)
    lo = lane < SW_DIM
    k_sw = pltpu.roll(k, SW_DIM, 1)
    v_sw = pltpu.roll(v, SW_DIM, 1)
    pairs = SW_GROUP // 2
    rows = pairs * C
    vals = []
    scores = []
    for g in range(SW_KV_HEADS):
        ksrc_a, ksrc_b = (k, k_sw) if g == 0 else (k_sw, k)
        vsrc_a, vsrc_b = (v, v_sw) if g == 0 else (v_sw, v)
        k_a = jnp.where(lo, ksrc_a, 0.0).astype(BF16)
        k_b = jnp.where(lo, 0.0, ksrc_b).astype(BF16)
        vals += [jnp.where(lo, vsrc_a, 0.0).astype(BF16), jnp.where(lo, 0.0, vsrc_b).astype(BF16)]
        qs = jnp.concatenate([q_ref[0, :, (g * pairs + j) * LANES:(g * pairs + j + 1) * LANES]
                              for j in range(pairs)], axis=0)
        for kx in (k_a, k_b):
            scores.append(lax.dot_general(kx, qs, (((1,), (1,)), ((), ())), preferred_element_type=F32))
    s = jnp.concatenate(scores, axis=1)
    key_pos = (c - WIN_CHUNKS) * C + lax.broadcasted_iota(jnp.int32, s.shape, 0)
    s = jnp.where(key_pos >= k_first, s, NEG_BIG)
    snk = sink_ref[...]
    m = jnp.maximum(jnp.max(s, axis=0, keepdims=True), snk)
    p = jnp.exp(s - m)
    den = jnp.sum(p, axis=0, keepdims=True) + jnp.exp(snk - m)
    p = (p * (1.0 / den)).astype(BF16)
    for g in range(SW_KV_HEADS):
        o_t = None
        for t in range(2):
            blk = p[:, (2 * g + t) * rows:(2 * g + t + 1) * rows]
            d = lax.dot_general(vals[2 * g + t], blk, (((0,), (0,)), ((), ())), preferred_element_type=F32)
            o_t = d if o_t is None else o_t + d
        for hb in range(pairs // 2):
            o2 = o_t[:, hb * LANES:(hb + 1) * LANES].T
            for jj in range(2):
                j = g * pairs + 2 * hb + jj
                o_ref[0, :, j * LANES:(j + 1) * LANES] = o2[jj * C:(jj + 1) * C, :].astype(o_ref.dtype)


def swa_attention(q, kv, sinks, *, n_chunks, chunk_offset, k_first):
    B = q.shape[0]
    pairs = SW_GROUP // 2
    sk = sinks.astype(F32).reshape(SW_KV_HEADS, pairs, 2).transpose(0, 2, 1)
    sk = jnp.repeat(sk.reshape(SW_KV_HEADS * 2 * pairs), CHUNK)[None, :]

    def kspec(d, blk):
        return pl.BlockSpec((1, CHUNK, LANES),
                            lambda b, c: (b, jnp.maximum(c + chunk_offset - d, 0), blk))

    return pl.pallas_call(
        functools.partial(_swa_kernel, chunk_offset=chunk_offset, k_first=k_first),
        grid=(B, n_chunks),
        in_specs=[pl.BlockSpec((1, CHUNK, SW_HEADS * SW_DIM), lambda b, c: (b, c, 0)),
                  kspec(2, 0), kspec(1, 0), kspec(0, 0), kspec(2, 1), kspec(1, 1), kspec(0, 1),
                  pl.BlockSpec((1, 2 * SW_KV_HEADS * pairs * CHUNK), lambda b, c: (0, 0))],
        out_specs=pl.BlockSpec((1, CHUNK, SW_HEADS * SW_DIM), lambda b, c: (b, c, 0)),
        out_shape=jax.ShapeDtypeStruct((B, n_chunks * CHUNK, SW_HEADS * SW_DIM), BF16),
        compiler_params=_cparams(("parallel", "parallel")), name="swa_attention")(
            q, kv, kv, kv, kv, kv, kv, sk)


def _mix_router_kernel(h_ref, o_ref, wo_ref, g_ref, rw_ref, rb_ref, h1_ref, u_ref, lg_ref):
    h1 = h_ref[...] + jnp.dot(o_ref[...], wo_ref[...], preferred_element_type=F32)
    h1_ref[...] = h1
    u = _rms(h1, g_ref[...])
    u_ref[...] = u.astype(u_ref.dtype)
    lg_ref[...] = _dot_hi(u, rw_ref[...]) + rb_ref[...]


def mix_router(h, o, wo, g, rw, rb, tm):
    M, D = h.shape
    assert M % tm == 0
    row = lambda i: (i, 0)
    return pl.pallas_call(
        _mix_router_kernel, grid=(M // tm,),
        in_specs=[pl.BlockSpec((tm, D), row), pl.BlockSpec((tm, D), row), _const_spec(wo.shape),
                  _const_spec((1, D)), _const_spec(rw.shape), _const_spec(rb.shape)],
        out_specs=[pl.BlockSpec((tm, D), row), pl.BlockSpec((tm, D), row), pl.BlockSpec((tm, LANES), row)],
        out_shape=[jax.ShapeDtypeStruct((M, D), F32), jax.ShapeDtypeStruct((M, D), BF16),
                   jax.ShapeDtypeStruct((M, LANES), F32)],
        compiler_params=_cparams(("parallel",)), name="mix_router")(h, o, wo, g, rw, rb)


def _expert_kernel(be_ref, nu_ref, x_ref, gate_ref, w1_ref, w3_ref, w2_ref, out_ref):
    i = pl.program_id(0)

    @pl.when(i < nu_ref[0])
    def _():
        x = x_ref[...]
        for (a, b) in _col_chunks(0, EXPERT_FF):
            x1 = jnp.dot(x, w1_ref[0, :, a:b], preferred_element_type=F32)
            x3 = jnp.dot(x, w3_ref[0, :, a:b], preferred_element_type=F32)
            hid = (x1 * jax.nn.sigmoid(x1) * x3).astype(BF16)
            y = jnp.dot(hid, w2_ref[0, a:b, :], preferred_element_type=F32)
            if a == 0:
                out_ref[...] = y
            else:
                out_ref[...] += y
        out_ref[...] = out_ref[...] * gate_ref[...]

    @pl.when(i >= nu_ref[0])
    def _():
        out_ref[...] = jnp.zeros(out_ref.shape, F32)


def expert_ffn(blk_expert, n_used, xs, gates, w1, w3, w2, blk):
    n_slot, D = xs.shape
    n_blk = n_slot // blk
    wspec = lambda shape: pl.BlockSpec((1,) + shape, lambda i, be, nu: (be[i], 0, 0),
                                       pipeline_mode=pl.Buffered(1))
    return pl.pallas_call(
        _expert_kernel,
        grid_spec=pltpu.PrefetchScalarGridSpec(
            num_scalar_prefetch=2, grid=(n_blk,),
            in_specs=[pl.BlockSpec((blk, D), lambda i, be, nu: (i, 0)),
                      pl.BlockSpec((blk, 1), lambda i, be, nu: (i, 0)),
                      wspec((D, EXPERT_FF)), wspec((D, EXPERT_FF)), wspec((EXPERT_FF, D))],
            out_specs=pl.BlockSpec((blk, D), lambda i, be, nu: (i, 0))),
        out_shape=jax.ShapeDtypeStruct((n_slot, D), F32),
        compiler_params=_cparams(("arbitrary",)), name="expert_ffn")(
            blk_expert, n_used, xs, gates, w1, w3, w2)


def _final_kernel(h_ref, ya_ref, yb_ref, g_ref, out_ref):
    out_ref[0] = _rms(h_ref[0] + (ya_ref[0] + yb_ref[0]), g_ref[...])


def final_norm(h, ya, yb, g, *, skip_rows, tm):
    B, L, D = h.shape
    assert skip_rows % tm == 0 and (L - skip_rows) % tm == 0
    off = skip_rows // tm
    return pl.pallas_call(
        _final_kernel, grid=(B, (L - skip_rows) // tm),
        in_specs=[pl.BlockSpec((1, tm, D), lambda b, i: (b, i + off, 0)),
                  pl.BlockSpec((1, tm, D), lambda b, i: (b, i + off, 0)),
                  pl.BlockSpec((1, tm, D), lambda b, i: (b, i + off, 0)),
                  pl.BlockSpec((1, D), lambda b, i: (0, 0))],
        out_specs=pl.BlockSpec((1, tm, D), lambda b, i: (b, i, 0)),
        out_shape=jax.ShapeDtypeStruct((B, L - skip_rows, D), F32),
        compiler_params=_cparams(("parallel", "parallel")), name="final_norm")(h, ya, yb, g)


def _route(logits, blk):
    n_tok = logits.shape[0]
    n_asg = n_tok * TOP_K
    top_val, top_idx = lax.top_k(logits, TOP_K)
    gates = jax.nn.softmax(top_val, axis=-1).reshape(-1)
    expert = top_idx.reshape(-1).astype(jnp.int32)
    onehot = (expert[:, None] == jnp.arange(N_EXPERTS, dtype=jnp.int32)[None, :]).astype(jnp.int32)
    csum = jnp.cumsum(onehot, axis=0)
    counts = csum[-1]
    rank = jnp.take_along_axis(csum, expert[:, None], axis=1)[:, 0] - 1
    start = jnp.cumsum(counts) - counts
    padded = (counts + blk - 1) // blk * blk
    pend = jnp.cumsum(padded)
    pstart = pend - padded
    dest = pstart[expert] + rank
    n_blk = (n_asg + N_EXPERTS * (blk - 1) + blk - 1) // blk
    n_slot = n_blk * blk
    blk_expert = jnp.minimum(jnp.searchsorted(pend, jnp.arange(n_blk, dtype=jnp.int32) * blk, side='right'),
                             N_EXPERTS - 1).astype(jnp.int32)
    order = jnp.argsort(expert, stable=True).astype(jnp.int32)
    e_slot = jnp.repeat(blk_expert, blk)
    r_slot = jnp.arange(n_slot, dtype=jnp.int32) - pstart[e_slot]
    used = r_slot < counts[e_slot]
    asg = order[jnp.clip(start[e_slot] + r_slot, 0, n_asg - 1)]
    slot_tok = jnp.where(used, asg // TOP_K, n_tok)
    slot_gate = jnp.where(used, gates[asg], 0.0)
    n_used = (pend[-1] // blk).astype(jnp.int32).reshape(1)
    return slot_tok, slot_gate, blk_expert, n_used, dest.reshape(n_tok, TOP_K)


def _moe(u_bf, logits, w1, w3, w2, blk):
    slot_tok, slot_gate, blk_expert, n_used, dest = _route(logits[:, :N_EXPERTS], blk)
    xs_pad = jnp.concatenate([u_bf, jnp.zeros((1, D_MODEL), u_bf.dtype)], axis=0)
    xs = xs_pad[slot_tok]
    yb = expert_ffn(blk_expert, n_used, xs, slot_gate[:, None], w1, w3, w2, blk)
    return yb[dest[:, 0]], yb[dest[:, 1]]


def _rope_tables(pos):
    half = SW_DIM // 2
    inv_freq = ROPE_THETA ** (-jnp.arange(half, dtype=F32) / half)
    ang = pos.astype(F32)[:, None] * inv_freq[None, :]
    cos, sin = jnp.cos(ang), jnp.sin(ang)
    cos = jnp.concatenate([cos, cos, cos, cos], axis=1)
    sin = jnp.concatenate([-sin, sin, -sin, sin], axis=1)
    return cos, sin


def _row_tile(m, l):
    for t in (640, 512, 384, 256, 128):
        if l % t == 0:
            return t
    t = 512
    while m % t:
        t //= 2
    return t


def kernel(x_prompt, x_sample, cache_sb_k, cache_sb_v, state_gdn, state_gdn_conv, cache_sw_k, cache_sw_v, meta_tokens, norm_mix, norm_ffn, norm_final, w_in_even, gdn_conv_w, gdn_a_log, gdn_dt_bias, gdn_norm, w_out_even, ffn_w1, ffn_w3, ffn_w2, w_in_odd, sw_sinks, w_out_odd, router_w, router_b, moe_w1, moe_w3, moe_w2):
    nb, seq, D = x_prompt.shape
    ns, tn, _ = x_sample.shape
    past = cache_sb_k.shape[1]
    lp = PREFIX + seq

    c_q, c_k, c_v, c_x, c_b, c_a, c_z = np.cumsum((0, SB_W, SB_W, SB_W, GDN_CONV_CH, GDN_HEADS, GDN_HEADS)).tolist()
    w_e = w_in_even.astype(F32)
    w_even = jnp.concatenate([
        w_e[:, c_q:c_k] * (SB_DIM ** -0.5), w_e[:, c_k:c_x], w_e[:, c_x:c_b], w_e[:, c_z:c_z + GDN_V],
        w_e[:, c_b:c_z], jnp.zeros((D, LANES - 2 * GDN_HEADS), F32)], axis=1).astype(BF16)
    qkv_w = 3 * SB_W
    even_groups = ((0, SB_W, (BF16,)), (SB_W, 2 * SB_W, (BF16, F32)), (2 * SB_W, qkv_w, (BF16, F32)),
                   (qkv_w, qkv_w + GDN_IN_W, (F32,)))
    w_o = w_in_odd.astype(F32)
    q_w = SW_HEADS * SW_DIM
    kv_w = SW_KV_HEADS * SW_DIM
    w_odd = jnp.concatenate([w_o[:, :q_w] * (SW_DIM ** -0.5), w_o[:, q_w:]], axis=1).astype(BF16)
    odd_groups = ((0, q_w, (BF16,)), (q_w, q_w + 2 * kv_w, (F32,)))
    woa = w_out_even[:SB_W].astype(BF16)
    wob = w_out_even[SB_W:].astype(BF16)
    w1, w3, w2 = ffn_w1.astype(BF16), ffn_w3.astype(BF16), ffn_w2.astype(BF16)
    wo_odd = w_out_odd.astype(BF16)
    rw = jnp.concatenate([router_w.astype(F32), jnp.zeros((D, LANES - N_EXPERTS), F32)], axis=1)
    rb = jnp.concatenate([router_b.astype(F32), jnp.zeros((LANES - N_EXPERTS,), F32)])[None, :]
    mw1, mw3, mw2 = moe_w1.astype(BF16), moe_w3.astype(BF16), moe_w2.astype(BF16)
    g_mix = norm_mix.astype(F32)
    g_ffn = norm_ffn.astype(F32)
    g_fin = norm_final.astype(F32)[None, :]

    def layers(h, B, L, first_valid, sb_fn, conv_buf, s0, pos, sw_fn, moe_blk):
        M = B * L
        tm = _row_tile(M, L)
        rep_b = 1 if L % tm == 0 else B
        valid = jnp.tile((jnp.arange(L) >= first_valid).astype(F32), rep_b)[:, None]
        q_bf, k_bf, k_f32, v_bf, v_f32, gin = norm_proj(h, g_mix[0:1], w_even, even_groups, tm)
        oa = sb_fn(q_bf.reshape(B, L, SB_W), k_bf.reshape(B, L, SB_W), v_bf.reshape(B, L, SB_W))
        conv8 = jnp.concatenate([jnp.zeros((B, 8 - (CONV_W - 1), GDN_CONV_CH), F32), conv_buf.astype(F32)], axis=1)
        gin3 = gin.reshape(B, L, GDN_IN_W)
        ob, s_new = gdn_mixer(gin3, conv8, s0.astype(F32), gdn_conv_w, gdn_a_log, gdn_dt_bias, gdn_norm,
                              first_valid=first_valid)
        conv_new = jnp.concatenate([conv_buf.astype(F32), gin3[:, :, :GDN_CONV_CH]], axis=1)[:, -(CONV_W - 1):]
        h = mix_ffn(h, oa.reshape(M, SB_W), ob.reshape(M, GDN_V), valid, woa, wob, g_ffn[0:1], w1, w3, w2, tm)
        cos, sin = _rope_tables(jnp.tile(pos, rep_b))
        q_sw, kv_sw = norm_proj(h, g_mix[1:2], w_odd, odd_groups, tm, rope=(cos, sin, q_w + kv_w))
        kv_sw = kv_sw.reshape(B, L, 2 * kv_w)
        o_sw = sw_fn(q_sw.reshape(B, L, q_w), kv_sw)
        h1, u_bf, logits = mix_router(h, o_sw.reshape(M, q_w), wo_odd, g_ffn[1:2], rw, rb, tm)
        ya, yb = _moe(u_bf, logits, mw1, mw3, mw2, moe_blk)
        return (h1.reshape(B, L, D), ya.reshape(B, L, D), yb.reshape(B, L, D),
                k_f32.reshape(B, L, SB_HEADS, SB_DIM), v_f32.reshape(B, L, SB_HEADS, SB_DIM),
                s_new, conv_new, kv_sw)

    dt = x_prompt.dtype
    hp = jnp.concatenate([jnp.zeros((nb, FIRST_VALID, D), dt),
                          jnp.broadcast_to(meta_tokens.astype(dt), (nb, N_META_TOK, D)), x_prompt], axis=1)
    pos_p = jnp.arange(lp, dtype=jnp.int32) - FIRST_VALID

    def sb_prompt(q, k, v):
        return sb_attention(q, k, v, 0, 0, tq=128, tk=512, q_start=0, k_first=FIRST_VALID)

    def sw_prompt(q, kv):
        return swa_attention(q, kv, sw_sinks, n_chunks=lp // CHUNK, chunk_offset=0, k_first=FIRST_VALID)

    h1p, yap, ybp, sbk_p, sbv_p, gdn_p, conv_p, kvsw_p = layers(
        hp.reshape(nb * lp, D), nb, lp, FIRST_VALID, sb_prompt,
        jnp.zeros((nb, CONV_W - 1, GDN_CONV_CH), dt), jnp.zeros((nb, GDN_HEADS, GDN_DK, GDN_DV), dt),
        pos_p, sw_prompt, 512)
    y_prompt = final_norm(h1p, yap, ybp, g_fin, skip_rows=PREFIX, tm=128)

    pos_s = N_META_TOK + past + jnp.arange(tn, dtype=jnp.int32)

    def sb_samp(q, k, v):
        kk = jnp.concatenate([cache_sb_k.reshape(ns, past, SB_W).astype(BF16), k], axis=1)
        vv = jnp.concatenate([cache_sb_v.reshape(ns, past, SB_W).astype(BF16), v], axis=1)
        return sb_attention(q, kk, vv, 0, 0, tq=tn, tk=64, q_start=past, k_first=0)

    def sw_samp(q, kv):
        kc = cache_sw_k.reshape(ns, -1, kv_w).astype(F32)
        vc = cache_sw_v.reshape(ns, -1, kv_w).astype(F32)
        kv_all = jnp.concatenate([jnp.concatenate([kc, vc], axis=2), kv], axis=1)
        return swa_attention(q, kv_all, sw_sinks, n_chunks=1, chunk_offset=WIN_CHUNKS, k_first=0)

    h1s, yas, ybs, sbk_s, sbv_s, gdn_s, conv_s, kvsw_s = layers(
        x_sample.reshape(ns * tn, D), ns, tn, 0, sb_samp, state_gdn_conv, state_gdn, pos_s, sw_samp, 128)
    y_sample = final_norm(h1s, yas, ybs, g_fin, skip_rows=0, tm=tn)

    def sw_split(kv):
        return (kv[:, :, :kv_w].reshape(kv.shape[0], -1, SW_KV_HEADS, SW_DIM),
                kv[:, :, kv_w:].reshape(kv.shape[0], -1, SW_KV_HEADS, SW_DIM))

    swk_p, swv_p = sw_split(kvsw_p[:, -WINDOW:])
    swk_s, swv_s = sw_split(kvsw_s)
    buf = cache_sw_k.shape[1]
    sw_k_s = jnp.concatenate([cache_sw_k, swk_s.astype(cache_sw_k.dtype)], axis=1)[:, -buf:]
    sw_v_s = jnp.concatenate([cache_sw_v, swv_s.astype(cache_sw_v.dtype)], axis=1)[:, -buf:]
    return (y_prompt, y_sample,
            sbk_p[:, FIRST_VALID:], sbv_p[:, FIRST_VALID:], sbk_s, sbv_s,
            gdn_p.astype(dt), gdn_s.astype(state_gdn.dtype), conv_p.astype(dt), conv_s.astype(dt),
            swk_p, swv_p, sw_k_s, sw_v_s)
```

```python
import functools
import math

import numpy as np
import jax
import jax.numpy as jnp
from jax import lax
from jax.experimental import pallas as pl
from jax.experimental.pallas import tpu as pltpu

F32 = jnp.float32
BF16 = jnp.bfloat16
HI = lax.Precision.HIGHEST

D_MODEL = 1024
CHUNK = 64
N_META_TOK = 16
PREFIX = 128
FIRST_VALID = PREFIX - N_META_TOK
RMS_EPS = 1e-6
L2_EPS = 1e-6

SB_HEADS = 8
SB_DIM = 64
SB_W = SB_HEADS * SB_DIM
GDN_HEADS = 4
GDN_DK = 128
GDN_DV = 128
CONV_W = 4
GDN_QK = GDN_HEADS * GDN_DK
GDN_V = GDN_HEADS * GDN_DV
GDN_CONV_CH = 2 * GDN_QK + GDN_V
GDN_IN_W = GDN_CONV_CH + GDN_V + 128
SW_HEADS = 16
SW_KV_HEADS = 2
SW_GROUP = SW_HEADS // SW_KV_HEADS
SW_DIM = 64
WINDOW = 128
WIN_CHUNKS = WINDOW // CHUNK
ROPE_THETA = 10000.0
D_FF = 2816
N_EXPERTS = 8
TOP_K = 2
EXPERT_FF = 3584

LANES = 128
VMEM_LIMIT = 56 * 1024 * 1024
NEG_BIG = -1e30
SB_EXIT = 120.0


def _cparams(sem):
    return pltpu.CompilerParams(dimension_semantics=sem, vmem_limit_bytes=VMEM_LIMIT)


def _const_spec(shape):
    nd = len(shape)
    return pl.BlockSpec(shape, lambda *_: (0,) * nd, pipeline_mode=pl.Buffered(1))


def _rms(x, g):
    ms = jnp.mean(x * x, axis=-1, keepdims=True)
    return x * lax.rsqrt(ms + RMS_EPS) * g


def _col_chunks(c0, c1, step=512):
    out = []
    while c0 < c1:
        out.append((c0, min(c0 + step, c1)))
        c0 = out[-1][1]
    return out


def _norm_proj_kernel(*refs, groups, rope_cols):
    if rope_cols:
        x_ref, g_ref, w_ref, cos_ref, sin_ref = refs[:5]
        outs = refs[5:]
    else:
        x_ref, g_ref, w_ref = refs[:3]
        outs = refs[3:]
    u = _rms(x_ref[...], g_ref[...]).astype(BF16)
    if rope_cols:
        cos = cos_ref[...]
        sin = sin_ref[...]
        lane = lax.broadcasted_iota(jnp.int32, cos.shape, 1)
        first_half = (lane % SW_DIM) < (SW_DIM // 2)
    oi = 0
    for (c0, c1, dtypes) in groups:
        for (a, b) in _col_chunks(c0, c1):
            y = jnp.dot(u, w_ref[:, a:b], preferred_element_type=F32)
            if a < rope_cols:
                pieces = []
                for p in range(0, b - a, LANES):
                    yp = y[:, p:p + LANES]
                    if a + p < rope_cols:
                        rot = jnp.where(first_half, pltpu.roll(yp, LANES - SW_DIM // 2, 1),
                                        pltpu.roll(yp, SW_DIM // 2, 1))
                        yp = yp * cos + rot * sin
                    pieces.append(yp)
                y = jnp.concatenate(pieces, axis=1) if len(pieces) > 1 else pieces[0]
            for k, dt in enumerate(dtypes):
                outs[oi + k][:, a - c0:b - c0] = y.astype(dt)
        oi += len(dtypes)


def norm_proj(x, g, w, groups, tm, rope=None):
    M, D = x.shape
    N = w.shape[1]
    assert M % tm == 0
    rope_cols = rope[2] if rope is not None else 0
    in_specs = [pl.BlockSpec((tm, D), lambda i: (i, 0)), _const_spec((1, D)), _const_spec((D, N))]
    args = [x, g, w]
    if rope is not None:
        reps = rope[0].shape[0] // tm
        assert rope[0].shape[0] == reps * tm and M % rope[0].shape[0] == 0
        in_specs += [pl.BlockSpec((tm, LANES), lambda i: (i % reps, 0))] * 2
        args += [rope[0], rope[1]]
    out_shape, out_specs = [], []
    for (c0, c1, dtypes) in groups:
        for dt in dtypes:
            out_shape.append(jax.ShapeDtypeStruct((M, c1 - c0), dt))
            out_specs.append(pl.BlockSpec((tm, c1 - c0), lambda i: (i, 0)))
    return pl.pallas_call(
        functools.partial(_norm_proj_kernel, groups=groups, rope_cols=rope_cols),
        grid=(M // tm,), in_specs=in_specs, out_specs=out_specs, out_shape=out_shape,
        compiler_params=_cparams(("parallel",)), name="norm_proj")(*args)


SB_HB = 2
SB_SUB = 256


def _sb_kernel(q_ref, k_ref, v_ref, o_ref, run_ref, acc_ref, *, tq, tk, q_start, k_first):
    i = pl.program_id(2)
    q0 = q_start + i * tq
    align = math.gcd(math.gcd(tq, tk), q_start) if q_start else math.gcd(tq, tk)
    lane_q = lax.broadcasted_iota(jnp.int32, (tq, LANES), 1)
    nh = 2 * SB_HB
    q_heads = []
    for lb in range(SB_HB):
        q = q_ref[0, :, lb * LANES:(lb + 1) * LANES]
        zero_q = jnp.zeros_like(q)
        q_heads += [jnp.where(lane_q < SB_DIM, q, zero_q), jnp.where(lane_q >= SB_DIM, q, zero_q)]
    qh = jnp.stack(q_heads)
    sub = min(tk, SB_SUB)
    n_sub = tk // sub
    r = lax.broadcasted_iota(jnp.int32, (sub, sub), 0)
    c = lax.broadcasted_iota(jnp.int32, (sub, sub), 1)
    later_mat = jnp.where(r > c, 1.0, 0.0).astype(BF16)
    run_ref[...] = jnp.zeros(run_ref.shape, F32)
    acc_ref[...] = jnp.zeros(acc_ref.shape, F32)

    def cond(carry):
        _, hi, alive = carry
        return jnp.logical_and(hi > 0, alive > 0)

    def body(carry):
        s, hi, _ = carry
        s = pl.multiple_of(s, align)
        qpos = q0 + lax.broadcasted_iota(jnp.int32, (tq, tk), 0)
        kpos = s + lax.broadcasted_iota(jnp.int32, (tq, tk), 1)
        lim = jnp.minimum(qpos, hi)
        if k_first:
            vis = (pltpu.bitcast(kpos - k_first, jnp.uint32)
                   < pltpu.bitcast(jnp.maximum(lim - k_first, 0), jnp.uint32))
        else:
            vis = kpos < lim
        vis = vis[None]
        kts = [k_ref[0, pl.ds(s, tk), lb * LANES:(lb + 1) * LANES] for lb in range(SB_HB)]
        vts = [v_ref[0, pl.ds(s, tk), lb * LANES:(lb + 1) * LANES] for lb in range(SB_HB)]
        kt = jnp.stack([kts[hh // 2] for hh in range(nh)])
        vt = jnp.stack([vts[hh // 2] for hh in range(nh)])
        z = lax.dot_general(qh, kt, (((2,), (2,)), ((0,), (0,))), preferred_element_type=F32)
        lk0 = -(jnp.maximum(z, 0.0) + jnp.log(1.0 + jnp.exp(-jnp.abs(z))))
        lk = jnp.where(vis, lk0, 0.0)
        lk_bf = lk.astype(BF16).reshape(nh * tq, tk)
        carry_sum = run_ref[...]
        parts = [None] * n_sub
        for b in range(n_sub - 1, -1, -1):
            cs = jnp.dot(lk_bf[:, b * sub:(b + 1) * sub], later_mat, preferred_element_type=F32)
            parts[b] = cs.reshape(nh, tq, sub) + carry_sum
            if b:
                carry_sum = carry_sum + jnp.sum(lk[:, :, b * sub:(b + 1) * sub], axis=-1, keepdims=True)
        later = jnp.concatenate(parts, axis=2) if n_sub > 1 else parts[0]
        w = jnp.where(vis, jnp.exp(z + lk0 + later), 0.0)
        acc_ref[...] += lax.dot_general(w.astype(BF16), vt, (((2,), (1,)), ((0,), (0,))),
                                        preferred_element_type=F32)
        run = run_ref[...] + jnp.sum(lk, axis=-1, keepdims=True)
        run_ref[...] = run
        alive = (jnp.max(run) > -SB_EXIT).astype(jnp.int32)
        return (jnp.maximum(s - tk, 0), s, alive)

    lax.while_loop(cond, body, (jnp.maximum(q0 + tq - tk, 0), q0 + tq, jnp.int32(1)))
    for lb in range(SB_HB):
        o_ref[0, :, lb * LANES:(lb + 1) * LANES] = jnp.where(
            lane_q < SB_DIM, acc_ref[2 * lb], acc_ref[2 * lb + 1]).astype(o_ref.dtype)


def sb_attention(q, k, v, k_off, v_off, *, tq, tk, q_start, k_first):
    B, Lq, _ = q.shape
    Lk = k.shape[1]
    wb = SB_HB * LANES
    assert Lq % tq == 0 and Lk >= tk and (q_start + tq - tk) % 16 == 0 and tk % min(tk, SB_SUB) == 0
    return pl.pallas_call(
        functools.partial(_sb_kernel, tq=tq, tk=tk, q_start=q_start, k_first=k_first),
        grid=(B, SB_W // wb, Lq // tq),
        in_specs=[pl.BlockSpec((1, tq, wb), lambda b, p, i: (b, i, p)),
                  pl.BlockSpec((1, Lk, wb), lambda b, p, i: (b, 0, k_off + p)),
                  pl.BlockSpec((1, Lk, wb), lambda b, p, i: (b, 0, v_off + p))],
        out_specs=pl.BlockSpec((1, tq, wb), lambda b, p, i: (b, i, p)),
        out_shape=jax.ShapeDtypeStruct((B, Lq, SB_W), BF16),
        scratch_shapes=[pltpu.VMEM((2 * SB_HB, tq, 1), F32), pltpu.VMEM((2 * SB_HB, tq, LANES), F32)],
        compiler_params=_cparams(("parallel", "parallel", "arbitrary")), name="sb_attention")(q, k, v)


def _dot_hi(a, b):
    return jnp.dot(a, b, preferred_element_type=F32, precision=HI)


def _dot_bf(a, b):
    return jnp.dot(a.astype(BF16), b.astype(BF16), preferred_element_type=F32)


def _split_bf(a, parts):
    out = []
    for _ in range(parts - 1):
        hi = a.astype(BF16)
        out.append(hi)
        a = a - hi.astype(F32)
    out.append(a.astype(BF16))
    return out


def _dot_split(a_parts, b_parts):
    acc = None
    for i, ap in enumerate(a_parts):
        for j, bp in enumerate(b_parts):
            if i + j < max(len(a_parts), len(b_parts)):
                d = jnp.dot(ap, bp, preferred_element_type=F32)
                acc = d if acc is None else acc + d
    return acc


def _bdot(a, b, dims):
    return lax.dot_general(a, b, (dims, ((0,), (0,))), preferred_element_type=F32)


def _bdot_split(a_parts, b_parts):
    acc = None
    for i, ap in enumerate(a_parts):
        for j, bp in enumerate(b_parts):
            if i + j < max(len(a_parts), len(b_parts)):
                d = _bdot(ap, bp, ((2,), (1,)))
                acc = d if acc is None else acc + d
    return acc


def _gdn_kernel(gin_ref, cbuf_ref, s0_ref, cw_ref, alog_ref, dt_ref, alogc_ref, dtc_ref, gn_ref,
                o_ref, s_out_ref, xbuf, state, *, first_valid, bb):
    c = pl.program_id(1)
    C = CHUNK
    H = GDN_HEADS

    @pl.when(c == 0)
    def _():
        for bi in range(bb):
            state[bi * H:(bi + 1) * H] = s0_ref[bi]
        xbuf[:, 0:8, :] = cbuf_ref[...]

    ri = lax.broadcasted_iota(jnp.int32, (C, C), 0)
    ci = lax.broadcasted_iota(jnp.int32, (C, C), 1)
    incl = (ri >= ci)[None]
    strict = (ri > ci)[None]
    tril = jnp.where(ri >= ci, 1.0, 0.0).astype(BF16)
    triu = jnp.where(ri <= ci, 1.0, 0.0).astype(BF16)
    row_t = c * C + lax.broadcasted_iota(jnp.int32, (C, LANES), 0)
    col_t = c * C + lax.broadcasted_iota(jnp.int32, (8, C), 1)

    qs, ks, vs, zs, gcs, grs, bcs = [], [], [], [], [], [], []
    for bi in range(bb):
        xbuf[bi, 8:8 + C, :] = gin_ref[bi, :, 0:GDN_CONV_CH]
        y = xbuf[bi, 8:8 + C, :] * cw_ref[CONV_W - 1:CONV_W, :]
        for d in range(1, CONV_W):
            y = y + xbuf[bi, 8 - d:8 - d + C, :] * cw_ref[CONV_W - 1 - d:CONV_W - d, :]
        xbuf[bi, 0:8, :] = xbuf[bi, C:C + 8, :]
        conv = y * jax.nn.sigmoid(y)
        ba = gin_ref[bi, :, GDN_CONV_CH + GDN_V:GDN_IN_W]
        beta_col = jnp.where(row_t >= first_valid, jax.nn.sigmoid(ba), 0.0)
        g_col = jnp.where(row_t >= first_valid,
                          -jnp.exp(alog_ref[...]) * jax.nn.softplus(ba + dt_ref[...]), 0.0)
        ba_t = ba.T
        g_row = jnp.where(col_t >= first_valid,
                          -jnp.exp(alogc_ref[...]) * jax.nn.softplus(ba_t[0:8, :] + dtc_ref[...]), 0.0)
        G_col = _dot_split([tril], _split_bf(g_col, 3))
        G_row = _dot_split(_split_bf(g_row, 3), [triu])
        for h in range(H):
            qs.append(conv[:, h * GDN_DK:(h + 1) * GDN_DK])
            ks.append(conv[:, GDN_QK + h * GDN_DK:GDN_QK + (h + 1) * GDN_DK])
            vs.append(conv[:, 2 * GDN_QK + h * GDN_DV:2 * GDN_QK + (h + 1) * GDN_DV])
            zs.append(gin_ref[bi, :, GDN_CONV_CH + h * GDN_DV:GDN_CONV_CH + (h + 1) * GDN_DV])
            gcs.append(G_col[:, H + h:H + h + 1])
            grs.append(G_row[H + h:H + h + 1, :])
            bcs.append(beta_col[:, h:h + 1])
    q = jnp.stack(qs)
    k = jnp.stack(ks)
    v = jnp.stack(vs)
    z = jnp.stack(zs)
    gc = jnp.stack(gcs)
    gr = jnp.stack(grs)
    bc = jnp.stack(bcs)
    q = q * lax.rsqrt(jnp.sum(q * q, axis=-1, keepdims=True) + L2_EPS) * (GDN_DK ** -0.5)
    k = k * lax.rsqrt(jnp.sum(k * k, axis=-1, keepdims=True) + L2_EPS)
    g_last = gc[:, C - 1:C, :]
    decay = jnp.where(incl, jnp.exp(jnp.minimum(gc - gr, 0.0)), 0.0)
    k_bf = k.astype(BF16)
    kk = _bdot(k_bf, k_bf, ((2,), (2,)))
    qk = _bdot(q.astype(BF16), k_bf, ((2,), (2,))) * decay
    pw = -jnp.where(strict, bc * kk * decay, 0.0)
    sol = jnp.concatenate([bc * v, (bc * jnp.exp(gc)) * k], axis=2)
    steps = int(math.log2(C))
    for s in range(steps):
        pw_parts = _split_bf(pw, 2)
        sol = sol + _bdot_split(pw_parts, _split_bf(sol, 2))
        if s + 1 < steps:
            pw = _bdot_split(pw_parts, pw_parts)
    u_v = sol[:, :, :GDN_DV]
    w_k = sol[:, :, GDN_DV:]
    S = state[...]
    S_bf = S.astype(BF16)
    u = u_v - _bdot(w_k.astype(BF16), S_bf, ((2,), (1,)))
    o = (_bdot((q * jnp.exp(gc)).astype(BF16), S_bf, ((2,), (1,)))
         + _bdot(qk.astype(BF16), u.astype(BF16), ((2,), (1,))))
    k_tail = k * jnp.exp(g_last - gc)
    k_tail_t = jnp.stack([k_tail[n].T for n in range(bb * H)]).astype(BF16)
    state[...] = jnp.exp(g_last) * S + _bdot(k_tail_t, u.astype(BF16), ((2,), (1,)))
    o = o * lax.rsqrt(jnp.mean(o * o, axis=-1, keepdims=True) + RMS_EPS) * gn_ref[...]
    o = (o * (z * jax.nn.sigmoid(z))).astype(o_ref.dtype)
    for bi in range(bb):
        for h in range(H):
            o_ref[bi, :, h * GDN_DV:(h + 1) * GDN_DV] = o[bi * H + h]

    @pl.when(c == pl.num_programs(1) - 1)
    def _():
        for bi in range(bb):
            s_out_ref[bi] = state[bi * H:(bi + 1) * H]


def gdn_mixer(gin, conv_buf8, s0, conv_w, a_log, dt_bias, gnorm, *, first_valid):
    B, L, _ = gin.shape
    assert L % CHUNK == 0
    bb = max(d for d in (8, 4, 2, 1) if B % d == 0)
    pad = jnp.zeros((LANES - 2 * GDN_HEADS,), F32)
    alog_row = jnp.concatenate([jnp.zeros((GDN_HEADS,), F32), a_log.astype(F32), pad])[None, :]
    dt_row = jnp.concatenate([jnp.zeros((GDN_HEADS,), F32), dt_bias.astype(F32), pad])[None, :]
    alog_col = alog_row[0, :8][:, None]
    dt_col = dt_row[0, :8][:, None]
    return pl.pallas_call(
        functools.partial(_gdn_kernel, first_valid=first_valid, bb=bb),
        grid=(B // bb, L // CHUNK),
        in_specs=[pl.BlockSpec((bb, CHUNK, GDN_IN_W), lambda b, c: (b, c, 0)),
                  pl.BlockSpec((bb, 8, GDN_CONV_CH), lambda b, c: (b, 0, 0)),
                  pl.BlockSpec((bb, GDN_HEADS, GDN_DK, GDN_DV), lambda b, c: (b, 0, 0, 0)),
                  pl.BlockSpec((CONV_W, GDN_CONV_CH), lambda b, c: (0, 0)),
                  pl.BlockSpec((1, LANES), lambda b, c: (0, 0)),
                  pl.BlockSpec((1, LANES), lambda b, c: (0, 0)),
                  pl.BlockSpec((8, 1), lambda b, c: (0, 0)),
                  pl.BlockSpec((8, 1), lambda b, c: (0, 0)),
                  pl.BlockSpec((1, GDN_DV), lambda b, c: (0, 0))],
        out_specs=[pl.BlockSpec((bb, CHUNK, GDN_V), lambda b, c: (b, c, 0)),
                   pl.BlockSpec((bb, GDN_HEADS, GDN_DK, GDN_DV), lambda b, c: (b, 0, 0, 0))],
        out_shape=[jax.ShapeDtypeStruct((B, L, GDN_V), BF16),
                   jax.ShapeDtypeStruct((B, GDN_HEADS, GDN_DK, GDN_DV), F32)],
        scratch_shapes=[pltpu.VMEM((bb, CHUNK + 8, GDN_CONV_CH), F32),
                        pltpu.VMEM((bb * GDN_HEADS, GDN_DK, GDN_DV), F32)],
        compiler_params=_cparams(("parallel", "arbitrary")), name="gdn_mixer")(
            gin, conv_buf8, s0, conv_w.astype(F32), alog_row, dt_row, alog_col, dt_col,
            gnorm.astype(F32)[None, :])


def _mix_ffn_kernel(h_ref, oa_ref, ob_ref, valid_ref, woa_ref, wob_ref, g_ref, w1_ref, w3_ref, w2_ref,
                    out_ref):
    h1 = (h_ref[...] + jnp.dot(oa_ref[...], woa_ref[...], preferred_element_type=F32)
          + jnp.dot(ob_ref[...], wob_ref[...], preferred_element_type=F32))
    u = _rms(h1, g_ref[...]).astype(BF16)
    out_ref[...] = h1
    for (a, b) in _col_chunks(0, D_FF):
        x1 = jnp.dot(u, w1_ref[:, a:b], preferred_element_type=F32)
        x3 = jnp.dot(u, w3_ref[:, a:b], preferred_element_type=F32)
        hid = (x1 * jax.nn.sigmoid(x1) * x3).astype(BF16)
        out_ref[...] += jnp.dot(hid, w2_ref[a:b, :], preferred_element_type=F32)
    out_ref[...] = jnp.where(valid_ref[...] > 0.0, out_ref[...], 0.0)


def mix_ffn(h, oa, ob, valid, woa, wob, g, w1, w3, w2, tm):
    M, D = h.shape
    reps = valid.shape[0] // tm
    assert M % tm == 0 and valid.shape[0] == reps * tm and M % valid.shape[0] == 0
    row = lambda i: (i, 0)
    return pl.pallas_call(
        _mix_ffn_kernel, grid=(M // tm,),
        in_specs=[pl.BlockSpec((tm, D), row), pl.BlockSpec((tm, SB_W), row), pl.BlockSpec((tm, GDN_V), row),
                  pl.BlockSpec((tm, 1), lambda i: (i % reps, 0)), _const_spec(woa.shape), _const_spec(wob.shape),
                  _const_spec((1, D)), _const_spec(w1.shape), _const_spec(w3.shape), _const_spec(w2.shape)],
        out_specs=pl.BlockSpec((tm, D), row),
        out_shape=jax.ShapeDtypeStruct((M, D), F32),
        compiler_params=_cparams(("parallel",)), name="mix_ffn")(h, oa, ob, valid, woa, wob, g, w1, w3, w2)


def _swa_kernel(q_ref, k0_ref, k1_ref, k2_ref, v0_ref, v1_ref, v2_ref, sink_ref, o_ref, *,
                chunk_offset, k_first):
    c = pl.program_id(1) + chunk_offset
    C = CHUNK
    nk = (WIN_CHUNKS + 1) * C
    k = jnp.concatenate([k0_ref[0], k1_ref[0], k2_ref[0]], axis=0)
    v = jnp.concatenate([v0_ref[0], v1_ref[0], v2_ref[0]], axis=0)
    lane = lax.broadcasted_iota(jnp.int32, (nk, LANES), 1)
    lo = lane < SW_DIM
    k_sw = pltpu.roll(k, SW_DIM, 1)
    v_sw = pltpu.roll(v, SW_DIM, 1)
    pairs = SW_GROUP // 2
    rows = pairs * C
    vals = []
    scores = []
    for g in range(SW_KV_HEADS):
        ksrc_a, ksrc_b = (k, k_sw) if g == 0 else (k_sw, k)
        vsrc_a, vsrc_b = (v, v_sw) if g == 0 else (v_sw, v)
        k_a = jnp.where(lo, ksrc_a, 0.0).astype(BF16)
        k_b = jnp.where(lo, 0.0, ksrc_b).astype(BF16)
        vals += [jnp.where(lo, vsrc_a, 0.0).astype(BF16), jnp.where(lo, 0.0, vsrc_b).astype(BF16)]
        qs = jnp.concatenate([q_ref[0, :, (g * pairs + j) * LANES:(g * pairs + j + 1) * LANES]
                              for j in range(pairs)], axis=0)
        for kx in (k_a, k_b):
            scores.append(lax.dot_general(kx, qs, (((1,), (1,)), ((), ())), preferred_element_type=F32))
    s = jnp.concatenate(scores, axis=1)
    key_pos = (c - WIN_CHUNKS) * C + lax.broadcasted_iota(jnp.int32, s.shape, 0)
    s = jnp.where(key_pos >= k_first, s, NEG_BIG)
    snk = sink_ref[...]
    m = jnp.maximum(jnp.max(s, axis=0, keepdims=True), snk)
    p = jnp.exp(s - m)
    den = jnp.sum(p, axis=0, keepdims=True) + jnp.exp(snk - m)
    p = (p * (1.0 / den)).astype(BF16)
    for g in range(SW_KV_HEADS):
        o_t = None
        for t in range(2):
            blk = p[:, (2 * g + t) * rows:(2 * g + t + 1) * rows]
            d = lax.dot_general(vals[2 * g + t], blk, (((0,), (0,)), ((), ())), preferred_element_type=F32)
            o_t = d if o_t is None else o_t + d
        for hb in range(pairs // 2):
            o2 = o_t[:, hb * LANES:(hb + 1) * LANES].T
            for jj in range(2):
                j = g * pairs + 2 * hb + jj
                o_ref[0, :, j * LANES:(j + 1) * LANES] = o2[jj * C:(jj + 1) * C, :].astype(o_ref.dtype)


def swa_attention(q, kv, sinks, *, n_chunks, chunk_offset, k_first):
    B = q.shape[0]
    pairs = SW_GROUP // 2
    sk = sinks.astype(F32).reshape(SW_KV_HEADS, pairs, 2).transpose(0, 2, 1)
    sk = jnp.repeat(sk.reshape(SW_KV_HEADS * 2 * pairs), CHUNK)[None, :]

    def kspec(d, blk):
        return pl.BlockSpec((1, CHUNK, LANES),
                            lambda b, c: (b, jnp.maximum(c + chunk_offset - d, 0), blk))

    return pl.pallas_call(
        functools.partial(_swa_kernel, chunk_offset=chunk_offset, k_first=k_first),
        grid=(B, n_chunks),
        in_specs=[pl.BlockSpec((1, CHUNK, SW_HEADS * SW_DIM), lambda b, c: (b, c, 0)),
                  kspec(2, 0), kspec(1, 0), kspec(0, 0), kspec(2, 1), kspec(1, 1), kspec(0, 1),
                  pl.BlockSpec((1, 2 * SW_KV_HEADS * pairs * CHUNK), lambda b, c: (0, 0))],
        out_specs=pl.BlockSpec((1, CHUNK, SW_HEADS * SW_DIM), lambda b, c: (b, c, 0)),
        out_shape=jax.ShapeDtypeStruct((B, n_chunks * CHUNK, SW_HEADS * SW_DIM), BF16),
        compiler_params=_cparams(("parallel", "parallel")), name="swa_attention")(
            q, kv, kv, kv, kv, kv, kv, sk)


def _pack_bf16_pairs(x):
    n = x.shape[1] // 2
    bits = pltpu.bitcast(x.astype(BF16).astype(F32), jnp.uint32)
    return bits[:, n:] | (bits[:, :n] >> 16)


def _unpack_bf16_pairs(p):
    lo = pltpu.bitcast(p << 16, F32)
    hi = pltpu.bitcast(p & jnp.uint32(0xFFFF0000), F32)
    return jnp.concatenate([lo, hi], axis=1).astype(BF16)


def _mix_router_kernel(h_ref, o_ref, wo_ref, g_ref, rw_ref, rb_ref, h1_ref, u_ref, lg_ref):
    h1 = h_ref[...] + jnp.dot(o_ref[...], wo_ref[...], preferred_element_type=F32)
    h1_ref[...] = h1
    u = _rms(h1, g_ref[...])
    u_ref[...] = _pack_bf16_pairs(u)
    lg_ref[...] = _dot_hi(u, rw_ref[...]) + rb_ref[...]


def mix_router(h, o, wo, g, rw, rb, tm):
    M, D = h.shape
    assert M % tm == 0
    row = lambda i: (i, 0)
    return pl.pallas_call(
        _mix_router_kernel, grid=(M // tm,),
        in_specs=[pl.BlockSpec((tm, D), row), pl.BlockSpec((tm, D), row), _const_spec(wo.shape),
                  _const_spec((1, D)), _const_spec(rw.shape), _const_spec(rb.shape)],
        out_specs=[pl.BlockSpec((tm, D), row), pl.BlockSpec((tm, D // 2), row), pl.BlockSpec((tm, LANES), row)],
        out_shape=[jax.ShapeDtypeStruct((M, D), F32), jax.ShapeDtypeStruct((M, D // 2), jnp.uint32),
                   jax.ShapeDtypeStruct((M, LANES), F32)],
        compiler_params=_cparams(("parallel",)), name="mix_router")(h, o, wo, g, rw, rb)


def _dispatch_kernel(dest_ref, u_ref, xs_in_ref, xs_ref, sem, *, tm):
    del xs_in_ref
    base = pl.program_id(0) * (TOP_K * tm)

    def row_copy(r, d):
        return pltpu.make_async_copy(u_ref.at[pl.ds(r, 1)], xs_ref.at[pl.ds(d, 1)], sem)

    copies = [row_copy(r, dest_ref[base + TOP_K * r + k]) for r in range(tm) for k in range(TOP_K)]
    for cp in copies:
        cp.start()
    for cp in copies:
        cp.wait()


def dispatch_rows(u_packed, dest_flat, n_slot, tm):
    M, W = u_packed.shape
    assert M % tm == 0
    return pl.pallas_call(
        functools.partial(_dispatch_kernel, tm=tm),
        grid_spec=pltpu.PrefetchScalarGridSpec(
            num_scalar_prefetch=1, grid=(M // tm,),
            in_specs=[pl.BlockSpec((tm, W), lambda i, d: (i, 0)), pl.BlockSpec(memory_space=pl.ANY)],
            out_specs=pl.BlockSpec(memory_space=pl.ANY),
            scratch_shapes=[pltpu.SemaphoreType.DMA(())]),
        out_shape=jax.ShapeDtypeStruct((n_slot, W), jnp.uint32),
        input_output_aliases={2: 0},
        compiler_params=_cparams(("arbitrary",)), name="dispatch_rows")(
            dest_flat, u_packed, jnp.zeros((n_slot, W), jnp.uint32))


def _expert_kernel(be_ref, nu_ref, x_ref, w1_ref, w3_ref, w2_ref, out_ref):
    i = pl.program_id(0)

    @pl.when(i < nu_ref[0])
    def _():
        x = _unpack_bf16_pairs(x_ref[...])
        for (a, b) in _col_chunks(0, EXPERT_FF):
            x1 = jnp.dot(x, w1_ref[0, :, a:b], preferred_element_type=F32)
            x3 = jnp.dot(x, w3_ref[0, :, a:b], preferred_element_type=F32)
            hid = (x1 * jax.nn.sigmoid(x1) * x3).astype(BF16)
            y = jnp.dot(hid, w2_ref[0, a:b, :], preferred_element_type=F32)
            if a == 0:
                out_ref[...] = y
            else:
                out_ref[...] += y

    @pl.when(i >= nu_ref[0])
    def _():
        out_ref[...] = jnp.zeros(out_ref.shape, F32)


def expert_ffn(blk_expert, n_used, xs, w1, w3, w2, blk):
    n_slot, W = xs.shape
    D = 2 * W
    n_blk = n_slot // blk
    wspec = lambda shape: pl.BlockSpec((1,) + shape, lambda i, be, nu: (be[i], 0, 0),
                                       pipeline_mode=pl.Buffered(1))
    return pl.pallas_call(
        _expert_kernel,
        grid_spec=pltpu.PrefetchScalarGridSpec(
            num_scalar_prefetch=2, grid=(n_blk,),
            in_specs=[pl.BlockSpec((blk, W), lambda i, be, nu: (i, 0)),
                      wspec((D, EXPERT_FF)), wspec((D, EXPERT_FF)), wspec((EXPERT_FF, D))],
            out_specs=pl.BlockSpec((blk, D), lambda i, be, nu: (i, 0))),
        out_shape=jax.ShapeDtypeStruct((n_slot, D), F32),
        compiler_params=_cparams(("arbitrary",)), name="expert_ffn")(
            blk_expert, n_used, xs, w1, w3, w2)


def _combine_kernel(dest_ref, h_ref, gate_ref, g_ref, y_hbm, out_ref, ybuf, sem, *, tm, rows_per_batch,
                    skip_rows):
    tok0 = pl.program_id(0) * rows_per_batch + skip_rows + pl.program_id(1) * tm

    def row_copy(r, k):
        d = dest_ref[TOP_K * (tok0 + r) + k]
        return pltpu.make_async_copy(y_hbm.at[pl.ds(d, 1)], ybuf.at[k, pl.ds(r, 1)], sem)

    copies = [row_copy(r, k) for r in range(tm) for k in range(TOP_K)]
    for cp in copies:
        cp.start()
    for cp in copies:
        cp.wait()
    gates = gate_ref[0]
    y = ybuf[0] * gates[:, 0:1] + ybuf[1] * gates[:, 1:2]
    out_ref[0] = _rms(h_ref[0] + y, g_ref[...])


def combine_norm(h, gates, y_slots, dest_flat, g, *, skip_rows, tm):
    B, L, D = h.shape
    assert skip_rows % tm == 0 and (L - skip_rows) % tm == 0
    off = skip_rows // tm
    return pl.pallas_call(
        functools.partial(_combine_kernel, tm=tm, rows_per_batch=L, skip_rows=skip_rows),
        grid_spec=pltpu.PrefetchScalarGridSpec(
            num_scalar_prefetch=1, grid=(B, (L - skip_rows) // tm),
            in_specs=[pl.BlockSpec((1, tm, D), lambda b, i, d: (b, i + off, 0)),
                      pl.BlockSpec((1, tm, TOP_K), lambda b, i, d: (b, i + off, 0)),
                      pl.BlockSpec((1, D), lambda b, i, d: (0, 0)),
                      pl.BlockSpec(memory_space=pl.ANY)],
            out_specs=pl.BlockSpec((1, tm, D), lambda b, i, d: (b, i, 0)),
            scratch_shapes=[pltpu.VMEM((TOP_K, tm, D), F32), pltpu.SemaphoreType.DMA(())]),
        out_shape=jax.ShapeDtypeStruct((B, L - skip_rows, D), F32),
        compiler_params=_cparams(("arbitrary", "arbitrary")), name="combine_norm")(
            dest_flat, h, gates, g, y_slots)


def _route(logits, blk):
    n_tok = logits.shape[0]
    n_asg = n_tok * TOP_K
    top_val, top_idx = lax.top_k(logits, TOP_K)
    gates = jax.nn.softmax(top_val, axis=-1)
    expert = top_idx.reshape(-1).astype(jnp.int32)
    onehot = (expert[:, None] == jnp.arange(N_EXPERTS, dtype=jnp.int32)[None, :]).astype(jnp.int32)
    csum = jnp.cumsum(onehot, axis=0)
    counts = csum[-1]
    rank = jnp.take_along_axis(csum, expert[:, None], axis=1)[:, 0] - 1
    padded = (counts + blk - 1) // blk * blk
    pend = jnp.cumsum(padded)
    pstart = pend - padded
    dest = pstart[expert] + rank
    n_blk = (n_asg + N_EXPERTS * (blk - 1) + blk - 1) // blk
    blk_expert = jnp.minimum(jnp.searchsorted(pend, jnp.arange(n_blk, dtype=jnp.int32) * blk, side='right'),
                             N_EXPERTS - 1).astype(jnp.int32)
    n_used = (pend[-1] // blk).astype(jnp.int32).reshape(1)
    return dest.astype(jnp.int32), gates, blk_expert, n_used, n_blk * blk


def _moe(u_packed, logits, w1, w3, w2, blk, tm):
    dest, gates, blk_expert, n_used, n_slot = _route(logits[:, :N_EXPERTS], blk)
    xs = dispatch_rows(u_packed, dest, n_slot, tm)
    return expert_ffn(blk_expert, n_used, xs, w1, w3, w2, blk), dest, gates


def _rope_tables(pos):
    half = SW_DIM // 2
    inv_freq = ROPE_THETA ** (-jnp.arange(half, dtype=F32) / half)
    ang = pos.astype(F32)[:, None] * inv_freq[None, :]
    cos, sin = jnp.cos(ang), jnp.sin(ang)
    cos = jnp.concatenate([cos, cos, cos, cos], axis=1)
    sin = jnp.concatenate([-sin, sin, -sin, sin], axis=1)
    return cos, sin


def _row_tile(m, l):
    for t in (640, 512, 384, 256, 128):
        if l % t == 0:
            return t
    t = 512
    while m % t:
        t //= 2
    return t


def kernel(x_prompt, x_sample, cache_sb_k, cache_sb_v, state_gdn, state_gdn_conv, cache_sw_k, cache_sw_v, meta_tokens, norm_mix, norm_ffn, norm_final, w_in_even, gdn_conv_w, gdn_a_log, gdn_dt_bias, gdn_norm, w_out_even, ffn_w1, ffn_w3, ffn_w2, w_in_odd, sw_sinks, w_out_odd, router_w, router_b, moe_w1, moe_w3, moe_w2):
    nb, seq, D = x_prompt.shape
    ns, tn, _ = x_sample.shape
    past = cache_sb_k.shape[1]
    lp = PREFIX + seq

    c_q, c_k, c_v, c_x, c_b, c_a, c_z = np.cumsum((0, SB_W, SB_W, SB_W, GDN_CONV_CH, GDN_HEADS, GDN_HEADS)).tolist()
    w_e = w_in_even.astype(F32)
    w_even = jnp.concatenate([
        w_e[:, c_q:c_k] * (SB_DIM ** -0.5), w_e[:, c_k:c_x], w_e[:, c_x:c_b], w_e[:, c_z:c_z + GDN_V],
        w_e[:, c_b:c_z], jnp.zeros((D, LANES - 2 * GDN_HEADS), F32)], axis=1).astype(BF16)
    qkv_w = 3 * SB_W
    even_groups = ((0, SB_W, (BF16,)), (SB_W, 2 * SB_W, (BF16, F32)), (2 * SB_W, qkv_w, (BF16, F32)),
                   (qkv_w, qkv_w + GDN_IN_W, (F32,)))
    w_o = w_in_odd.astype(F32)
    q_w = SW_HEADS * SW_DIM
    kv_w = SW_KV_HEADS * SW_DIM
    w_odd = jnp.concatenate([w_o[:, :q_w] * (SW_DIM ** -0.5), w_o[:, q_w:]], axis=1).astype(BF16)
    odd_groups = ((0, q_w, (BF16,)), (q_w, q_w + 2 * kv_w, (F32,)))
    woa = w_out_even[:SB_W].astype(BF16)
    wob = w_out_even[SB_W:].astype(BF16)
    w1, w3, w2 = ffn_w1.astype(BF16), ffn_w3.astype(BF16), ffn_w2.astype(BF16)
    wo_odd = w_out_odd.astype(BF16)
    rw = jnp.concatenate([router_w.astype(F32), jnp.zeros((D, LANES - N_EXPERTS), F32)], axis=1)
    rb = jnp.concatenate([router_b.astype(F32), jnp.zeros((LANES - N_EXPERTS,), F32)])[None, :]
    mw1, mw3, mw2 = moe_w1.astype(BF16), moe_w3.astype(BF16), moe_w2.astype(BF16)
    g_mix = norm_mix.astype(F32)
    g_ffn = norm_ffn.astype(F32)
    g_fin = norm_final.astype(F32)[None, :]

    def layers(h, B, L, first_valid, sb_fn, conv_buf, s0, pos, sw_fn, moe_blk):
        M = B * L
        tm = _row_tile(M, L)
        rep_b = 1 if L % tm == 0 else B
        valid = jnp.tile((jnp.arange(L) >= first_valid).astype(F32), rep_b)[:, None]
        q_bf, k_bf, k_f32, v_bf, v_f32, gin = norm_proj(h, g_mix[0:1], w_even, even_groups, tm)
        oa = sb_fn(q_bf.reshape(B, L, SB_W), k_bf.reshape(B, L, SB_W), v_bf.reshape(B, L, SB_W))
        conv8 = jnp.concatenate([jnp.zeros((B, 8 - (CONV_W - 1), GDN_CONV_CH), F32), conv_buf.astype(F32)], axis=1)
        gin3 = gin.reshape(B, L, GDN_IN_W)
        ob, s_new = gdn_mixer(gin3, conv8, s0.astype(F32), gdn_conv_w, gdn_a_log, gdn_dt_bias, gdn_norm,
                              first_valid=first_valid)
        conv_new = jnp.concatenate([conv_buf.astype(F32), gin3[:, :, :GDN_CONV_CH]], axis=1)[:, -(CONV_W - 1):]
        h = mix_ffn(h, oa.reshape(M, SB_W), ob.reshape(M, GDN_V), valid, woa, wob, g_ffn[0:1], w1, w3, w2, tm)
        cos, sin = _rope_tables(jnp.tile(pos, rep_b))
        q_sw, kv_sw = norm_proj(h, g_mix[1:2], w_odd, odd_groups, tm, rope=(cos, sin, q_w + kv_w))
        kv_sw = kv_sw.reshape(B, L, 2 * kv_w)
        o_sw = sw_fn(q_sw.reshape(B, L, q_w), kv_sw)
        h1, u_packed, logits = mix_router(h, o_sw.reshape(M, q_w), wo_odd, g_ffn[1:2], rw, rb, tm)
        y_slots, dest, gates = _moe(u_packed, logits, mw1, mw3, mw2, moe_blk, math.gcd(M, 256))
        moe_out = (h1.reshape(B, L, D), gates.reshape(B, L, TOP_K), y_slots, dest)
        return (moe_out, k_f32.reshape(B, L, SB_HEADS, SB_DIM), v_f32.reshape(B, L, SB_HEADS, SB_DIM),
                s_new, conv_new, kv_sw)

    dt = x_prompt.dtype
    hp = jnp.concatenate([jnp.zeros((nb, FIRST_VALID, D), dt),
                          jnp.broadcast_to(meta_tokens.astype(dt), (nb, N_META_TOK, D)), x_prompt], axis=1)
    pos_p = jnp.arange(lp, dtype=jnp.int32) - FIRST_VALID

    def sb_prompt(q, k, v):
        return sb_attention(q, k, v, 0, 0, tq=128, tk=512, q_start=0, k_first=FIRST_VALID)

    def sw_prompt(q, kv):
        return swa_attention(q, kv, sw_sinks, n_chunks=lp // CHUNK, chunk_offset=0, k_first=FIRST_VALID)

    moe_p, sbk_p, sbv_p, gdn_p, conv_p, kvsw_p = layers(
        hp.reshape(nb * lp, D), nb, lp, FIRST_VALID, sb_prompt,
        jnp.zeros((nb, CONV_W - 1, GDN_CONV_CH), dt), jnp.zeros((nb, GDN_HEADS, GDN_DK, GDN_DV), dt),
        pos_p, sw_prompt, 512)
    y_prompt = combine_norm(*moe_p, g_fin, skip_rows=PREFIX, tm=128)

    pos_s = N_META_TOK + past + jnp.arange(tn, dtype=jnp.int32)

    def sb_samp(q, k, v):
        kk = jnp.concatenate([cache_sb_k.reshape(ns, past, SB_W).astype(BF16), k], axis=1)
        vv = jnp.concatenate([cache_sb_v.reshape(ns, past, SB_W).astype(BF16), v], axis=1)
        return sb_attention(q, kk, vv, 0, 0, tq=tn, tk=64, q_start=past, k_first=0)

    def sw_samp(q, kv):
        kc = cache_sw_k.reshape(ns, -1, kv_w).astype(F32)
        vc = cache_sw_v.reshape(ns, -1, kv_w).astype(F32)
        kv_all = jnp.concatenate([jnp.concatenate([kc, vc], axis=2), kv], axis=1)
        return swa_attention(q, kv_all, sw_sinks, n_chunks=1, chunk_offset=WIN_CHUNKS, k_first=0)

    moe_s, sbk_s, sbv_s, gdn_s, conv_s, kvsw_s = layers(
        x_sample.reshape(ns * tn, D), ns, tn, 0, sb_samp, state_gdn_conv, state_gdn, pos_s, sw_samp, 128)
    y_sample = combine_norm(*moe_s, g_fin, skip_rows=0, tm=tn)

    def sw_split(kv):
        return (kv[:, :, :kv_w].reshape(kv.shape[0], -1, SW_KV_HEADS, SW_DIM),
                kv[:, :, kv_w:].reshape(kv.shape[0], -1, SW_KV_HEADS, SW_DIM))

    swk_p, swv_p = sw_split(kvsw_p[:, -WINDOW:])
    swk_s, swv_s = sw_split(kvsw_s)
    buf = cache_sw_k.shape[1]
    sw_k_s = jnp.concatenate([cache_sw_k, swk_s.astype(cache_sw_k.dtype)], axis=1)[:, -buf:]
    sw_v_s = jnp.concatenate([cache_sw_v, swv_s.astype(cache_sw_v.dtype)], axis=1)[:, -buf:]
    return (y_prompt, y_sample,
            sbk_p[:, FIRST_VALID:], sbv_p[:, FIRST_VALID:], sbk_s, sbv_s,
            gdn_p.astype(dt), gdn_s.astype(state_gdn.dtype), conv_p.astype(dt), conv_s.astype(dt),
            swk_p, swv_p, sw_k_s, sw_v_s)
```

```python
import functools
import math

import numpy as np
import jax
import jax.numpy as jnp
from jax import lax
from jax.experimental import pallas as pl
from jax.experimental.pallas import tpu as pltpu

F32 = jnp.float32
BF16 = jnp.bfloat16
HI = lax.Precision.HIGHEST

D_MODEL = 1024
CHUNK = 64
N_META_TOK = 16
PREFIX = 128
FIRST_VALID = PREFIX - N_META_TOK
RMS_EPS = 1e-6
L2_EPS = 1e-6

SB_HEADS = 8
SB_DIM = 64
SB_W = SB_HEADS * SB_DIM
GDN_HEADS = 4
GDN_DK = 128
GDN_DV = 128
CONV_W = 4
GDN_QK = GDN_HEADS * GDN_DK
GDN_V = GDN_HEADS * GDN_DV
GDN_CONV_CH = 2 * GDN_QK + GDN_V
GDN_IN_W = GDN_CONV_CH + GDN_V + 128
SW_HEADS = 16
SW_KV_HEADS = 2
SW_GROUP = SW_HEADS // SW_KV_HEADS
SW_DIM = 64
WINDOW = 128
WIN_CHUNKS = WINDOW // CHUNK
ROPE_THETA = 10000.0
D_FF = 2816
N_EXPERTS = 8
TOP_K = 2
EXPERT_FF = 3584

LANES = 128
VMEM_LIMIT = 56 * 1024 * 1024
NEG_BIG = -1e30
SB_EXIT = 120.0


def _cparams(sem):
    return pltpu.CompilerParams(dimension_semantics=sem, vmem_limit_bytes=VMEM_LIMIT)


def _const_spec(shape):
    nd = len(shape)
    return pl.BlockSpec(shape, lambda *_: (0,) * nd, pipeline_mode=pl.Buffered(1))


def _rms(x, g):
    ms = jnp.mean(x * x, axis=-1, keepdims=True)
    return x * lax.rsqrt(ms + RMS_EPS) * g


def _col_chunks(c0, c1, step=512):
    out = []
    while c0 < c1:
        out.append((c0, min(c0 + step, c1)))
        c0 = out[-1][1]
    return out


def _norm_proj_kernel(*refs, groups, rope_cols):
    if rope_cols:
        x_ref, g_ref, w_ref, cos_ref, sin_ref = refs[:5]
        outs = refs[5:]
    else:
        x_ref, g_ref, w_ref = refs[:3]
        outs = refs[3:]
    u = _rms(x_ref[...], g_ref[...]).astype(BF16)
    if rope_cols:
        cos = cos_ref[...]
        sin = sin_ref[...]
        lane = lax.broadcasted_iota(jnp.int32, cos.shape, 1)
        first_half = (lane % SW_DIM) < (SW_DIM // 2)
    oi = 0
    for (c0, c1, dtypes) in groups:
        for (a, b) in _col_chunks(c0, c1):
            y = jnp.dot(u, w_ref[:, a:b], preferred_element_type=F32)
            if a < rope_cols:
                pieces = []
                for p in range(0, b - a, LANES):
                    yp = y[:, p:p + LANES]
                    if a + p < rope_cols:
                        rot = jnp.where(first_half, pltpu.roll(yp, LANES - SW_DIM // 2, 1),
                                        pltpu.roll(yp, SW_DIM // 2, 1))
                        yp = yp * cos + rot * sin
                    pieces.append(yp)
                y = jnp.concatenate(pieces, axis=1) if len(pieces) > 1 else pieces[0]
            for k, dt in enumerate(dtypes):
                outs[oi + k][:, a - c0:b - c0] = y.astype(dt)
        oi += len(dtypes)


def norm_proj(x, g, w, groups, tm, rope=None):
    M, D = x.shape
    N = w.shape[1]
    assert M % tm == 0
    rope_cols = rope[2] if rope is not None else 0
    in_specs = [pl.BlockSpec((tm, D), lambda i: (i, 0)), _const_spec((1, D)), _const_spec((D, N))]
    args = [x, g, w]
    if rope is not None:
        reps = rope[0].shape[0] // tm
        assert rope[0].shape[0] == reps * tm and M % rope[0].shape[0] == 0
        in_specs += [pl.BlockSpec((tm, LANES), lambda i: (i % reps, 0))] * 2
        args += [rope[0], rope[1]]
    out_shape, out_specs = [], []
    for (c0, c1, dtypes) in groups:
        for dt in dtypes:
            out_shape.append(jax.ShapeDtypeStruct((M, c1 - c0), dt))
            out_specs.append(pl.BlockSpec((tm, c1 - c0), lambda i: (i, 0)))
    return pl.pallas_call(
        functools.partial(_norm_proj_kernel, groups=groups, rope_cols=rope_cols),
        grid=(M // tm,), in_specs=in_specs, out_specs=out_specs, out_shape=out_shape,
        compiler_params=_cparams(("parallel",)), name="norm_proj")(*args)


SB_HB = 2
SB_SUB = 256


def _sb_kernel(q_ref, k_ref, v_ref, o_ref, run_ref, acc_ref, *, tq, tk, q_start, k_first):
    i = pl.program_id(2)
    q0 = q_start + i * tq
    align = math.gcd(math.gcd(tq, tk), q_start) if q_start else math.gcd(tq, tk)
    lane_q = lax.broadcasted_iota(jnp.int32, (tq, LANES), 1)
    nh = 2 * SB_HB
    q_heads = []
    for lb in range(SB_HB):
        q = q_ref[0, :, lb * LANES:(lb + 1) * LANES]
        zero_q = jnp.zeros_like(q)
        q_heads += [jnp.where(lane_q < SB_DIM, q, zero_q), jnp.where(lane_q >= SB_DIM, q, zero_q)]
    qh = jnp.stack(q_heads)
    sub = min(tk, SB_SUB)
    n_sub = tk // sub
    r = lax.broadcasted_iota(jnp.int32, (sub, sub), 0)
    c = lax.broadcasted_iota(jnp.int32, (sub, sub), 1)
    later_mat = jnp.where(r > c, 1.0, 0.0).astype(BF16)
    run_ref[...] = jnp.zeros(run_ref.shape, F32)
    acc_ref[...] = jnp.zeros(acc_ref.shape, F32)

    def cond(carry):
        _, hi, alive = carry
        return jnp.logical_and(hi > 0, alive > 0)

    def body(carry):
        s, hi, _ = carry
        s = pl.multiple_of(s, align)
        qpos = q0 + lax.broadcasted_iota(jnp.int32, (tq, tk), 0)
        kpos = s + lax.broadcasted_iota(jnp.int32, (tq, tk), 1)
        lim = jnp.minimum(qpos, hi)
        if k_first:
            vis = (pltpu.bitcast(kpos - k_first, jnp.uint32)
                   < pltpu.bitcast(jnp.maximum(lim - k_first, 0), jnp.uint32))
        else:
            vis = kpos < lim
        vis = vis[None]
        kts = [k_ref[0, pl.ds(s, tk), lb * LANES:(lb + 1) * LANES] for lb in range(SB_HB)]
        vts = [v_ref[0, pl.ds(s, tk), lb * LANES:(lb + 1) * LANES] for lb in range(SB_HB)]
        kt = jnp.stack([kts[hh // 2] for hh in range(nh)])
        vt = jnp.stack([vts[hh // 2] for hh in range(nh)])
        z = lax.dot_general(qh, kt, (((2,), (2,)), ((0,), (0,))), preferred_element_type=F32)
        lk0 = -(jnp.maximum(z, 0.0) + jnp.log(1.0 + jnp.exp(-jnp.abs(z))))
        lk = jnp.where(vis, lk0, 0.0)
        lk_bf = lk.astype(BF16).reshape(nh * tq, tk)
        carry_sum = run_ref[...]
        parts = [None] * n_sub
        for b in range(n_sub - 1, -1, -1):
            cs = jnp.dot(lk_bf[:, b * sub:(b + 1) * sub], later_mat, preferred_element_type=F32)
            parts[b] = cs.reshape(nh, tq, sub) + carry_sum
            if b:
                carry_sum = carry_sum + jnp.sum(lk[:, :, b * sub:(b + 1) * sub], axis=-1, keepdims=True)
        later = jnp.concatenate(parts, axis=2) if n_sub > 1 else parts[0]
        w = jnp.where(vis, jnp.exp(z + lk0 + later), 0.0)
        acc_ref[...] += lax.dot_general(w.astype(BF16), vt, (((2,), (1,)), ((0,), (0,))),
                                        preferred_element_type=F32)
        run = run_ref[...] + jnp.sum(lk, axis=-1, keepdims=True)
        run_ref[...] = run
        alive = (jnp.max(run) > -SB_EXIT).astype(jnp.int32)
        return (jnp.maximum(s - tk, 0), s, alive)

    lax.while_loop(cond, body, (jnp.maximum(q0 + tq - tk, 0), q0 + tq, jnp.int32(1)))
    for lb in range(SB_HB):
        o_ref[0, :, lb * LANES:(lb + 1) * LANES] = jnp.where(
            lane_q < SB_DIM, acc_ref[2 * lb], acc_ref[2 * lb + 1]).astype(o_ref.dtype)


def sb_attention(q, k, v, k_off, v_off, *, tq, tk, q_start, k_first):
    B, Lq, _ = q.shape
    Lk = k.shape[1]
    wb = SB_HB * LANES
    assert Lq % tq == 0 and Lk >= tk and (q_start + tq - tk) % 16 == 0 and tk % min(tk, SB_SUB) == 0
    return pl.pallas_call(
        functools.partial(_sb_kernel, tq=tq, tk=tk, q_start=q_start, k_first=k_first),
        grid=(B, SB_W // wb, Lq // tq),
        in_specs=[pl.BlockSpec((1, tq, wb), lambda b, p, i: (b, i, p)),
                  pl.BlockSpec((1, Lk, wb), lambda b, p, i: (b, 0, k_off + p)),
                  pl.BlockSpec((1, Lk, wb), lambda b, p, i: (b, 0, v_off + p))],
        out_specs=pl.BlockSpec((1, tq, wb), lambda b, p, i: (b, i, p)),
        out_shape=jax.ShapeDtypeStruct((B, Lq, SB_W), BF16),
        scratch_shapes=[pltpu.VMEM((2 * SB_HB, tq, 1), F32), pltpu.VMEM((2 * SB_HB, tq, LANES), F32)],
        compiler_params=_cparams(("parallel", "parallel", "arbitrary")), name="sb_attention")(q, k, v)


def _dot_hi(a, b):
    return jnp.dot(a, b, preferred_element_type=F32, precision=HI)


def _dot_bf(a, b):
    return jnp.dot(a.astype(BF16), b.astype(BF16), preferred_element_type=F32)


def _split_bf(a, parts):
    out = []
    for _ in range(parts - 1):
        hi = a.astype(BF16)
        out.append(hi)
        a = a - hi.astype(F32)
    out.append(a.astype(BF16))
    return out


def _dot_split(a_parts, b_parts):
    acc = None
    for i, ap in enumerate(a_parts):
        for j, bp in enumerate(b_parts):
            if i + j < max(len(a_parts), len(b_parts)):
                d = jnp.dot(ap, bp, preferred_element_type=F32)
                acc = d if acc is None else acc + d
    return acc


def _bdot(a, b, dims):
    return lax.dot_general(a, b, (dims, ((0,), (0,))), preferred_element_type=F32)


def _bdot_split(a_parts, b_parts):
    acc = None
    for i, ap in enumerate(a_parts):
        for j, bp in enumerate(b_parts):
            if i + j < max(len(a_parts), len(b_parts)):
                d = _bdot(ap, bp, ((2,), (1,)))
                acc = d if acc is None else acc + d
    return acc


def _gdn_kernel(gin_ref, cbuf_ref, s0_ref, cw_ref, alog_ref, dt_ref, alogc_ref, dtc_ref, gn_ref,
                o_ref, s_out_ref, xbuf, state, *, first_valid, bb):
    c = pl.program_id(1)
    C = CHUNK
    H = GDN_HEADS

    @pl.when(c == 0)
    def _():
        for bi in range(bb):
            state[bi * H:(bi + 1) * H] = s0_ref[bi]
        xbuf[:, 0:8, :] = cbuf_ref[...]

    ri = lax.broadcasted_iota(jnp.int32, (C, C), 0)
    ci = lax.broadcasted_iota(jnp.int32, (C, C), 1)
    incl = (ri >= ci)[None]
    strict = (ri > ci)[None]
    tril = jnp.where(ri >= ci, 1.0, 0.0).astype(BF16)
    triu = jnp.where(ri <= ci, 1.0, 0.0).astype(BF16)
    row_t = c * C + lax.broadcasted_iota(jnp.int32, (C, LANES), 0)
    col_t = c * C + lax.broadcasted_iota(jnp.int32, (8, C), 1)

    qs, ks, vs, zs, gcs, grs, bcs = [], [], [], [], [], [], []
    for bi in range(bb):
        xbuf[bi, 8:8 + C, :] = gin_ref[bi, :, 0:GDN_CONV_CH]
        y = xbuf[bi, 8:8 + C, :] * cw_ref[CONV_W - 1:CONV_W, :]
        for d in range(1, CONV_W):
            y = y + xbuf[bi, 8 - d:8 - d + C, :] * cw_ref[CONV_W - 1 - d:CONV_W - d, :]
        xbuf[bi, 0:8, :] = xbuf[bi, C:C + 8, :]
        conv = y * jax.nn.sigmoid(y)
        ba = gin_ref[bi, :, GDN_CONV_CH + GDN_V:GDN_IN_W]
        beta_col = jnp.where(row_t >= first_valid, jax.nn.sigmoid(ba), 0.0)
        g_col = jnp.where(row_t >= first_valid,
                          -jnp.exp(alog_ref[...]) * jax.nn.softplus(ba + dt_ref[...]), 0.0)
        ba_t = ba.T
        g_row = jnp.where(col_t >= first_valid,
                          -jnp.exp(alogc_ref[...]) * jax.nn.softplus(ba_t[0:8, :] + dtc_ref[...]), 0.0)
        G_col = _dot_split([tril], _split_bf(g_col, 3))
        G_row = _dot_split(_split_bf(g_row, 3), [triu])
        for h in range(H):
            qs.append(conv[:, h * GDN_DK:(h + 1) * GDN_DK])
            ks.append(conv[:, GDN_QK + h * GDN_DK:GDN_QK + (h + 1) * GDN_DK])
            vs.append(conv[:, 2 * GDN_QK + h * GDN_DV:2 * GDN_QK + (h + 1) * GDN_DV])
            zs.append(gin_ref[bi, :, GDN_CONV_CH + h * GDN_DV:GDN_CONV_CH + (h + 1) * GDN_DV])
            gcs.append(G_col[:, H + h:H + h + 1])
            grs.append(G_row[H + h:H + h + 1, :])
            bcs.append(beta_col[:, h:h + 1])
    q = jnp.stack(qs)
    k = jnp.stack(ks)
    v = jnp.stack(vs)
    z = jnp.stack(zs)
    gc = jnp.stack(gcs)
    gr = jnp.stack(grs)
    bc = jnp.stack(bcs)
    q = q * lax.rsqrt(jnp.sum(q * q, axis=-1, keepdims=True) + L2_EPS) * (GDN_DK ** -0.5)
    k = k * lax.rsqrt(jnp.sum(k * k, axis=-1, keepdims=True) + L2_EPS)
    g_last = gc[:, C - 1:C, :]
    decay = jnp.where(incl, jnp.exp(jnp.minimum(gc - gr, 0.0)), 0.0)
    k_bf = k.astype(BF16)
    kk = _bdot(k_bf, k_bf, ((2,), (2,)))
    qk = _bdot(q.astype(BF16), k_bf, ((2,), (2,))) * decay
    pw = -jnp.where(strict, bc * kk * decay, 0.0)
    sol = jnp.concatenate([bc * v, (bc * jnp.exp(gc)) * k], axis=2)
    steps = int(math.log2(C))
    for s in range(steps):
        pw_parts = _split_bf(pw, 2)
        sol = sol + _bdot_split(pw_parts, _split_bf(sol, 2))
        if s + 1 < steps:
            pw = _bdot_split(pw_parts, pw_parts)
    u_v = sol[:, :, :GDN_DV]
    w_k = sol[:, :, GDN_DV:]
    S = state[...]
    S_bf = S.astype(BF16)
    u = u_v - _bdot(w_k.astype(BF16), S_bf, ((2,), (1,)))
    o = (_bdot((q * jnp.exp(gc)).astype(BF16), S_bf, ((2,), (1,)))
         + _bdot(qk.astype(BF16), u.astype(BF16), ((2,), (1,))))
    k_tail = k * jnp.exp(g_last - gc)
    k_tail_t = jnp.stack([k_tail[n].T for n in range(bb * H)]).astype(BF16)
    state[...] = jnp.exp(g_last) * S + _bdot(k_tail_t, u.astype(BF16), ((2,), (1,)))
    o = o * lax.rsqrt(jnp.mean(o * o, axis=-1, keepdims=True) + RMS_EPS) * gn_ref[...]
    o = (o * (z * jax.nn.sigmoid(z))).astype(o_ref.dtype)
    for bi in range(bb):
        for h in range(H):
            o_ref[bi, :, h * GDN_DV:(h + 1) * GDN_DV] = o[bi * H + h]

    @pl.when(c == pl.num_programs(1) - 1)
    def _():
        for bi in range(bb):
            s_out_ref[bi] = state[bi * H:(bi + 1) * H]


def gdn_mixer(gin, conv_buf8, s0, conv_w, a_log, dt_bias, gnorm, *, first_valid):
    B, L, _ = gin.shape
    assert L % CHUNK == 0
    bb = max(d for d in (8, 4, 2, 1) if B % d == 0)
    pad = jnp.zeros((LANES - 2 * GDN_HEADS,), F32)
    alog_row = jnp.concatenate([jnp.zeros((GDN_HEADS,), F32), a_log.astype(F32), pad])[None, :]
    dt_row = jnp.concatenate([jnp.zeros((GDN_HEADS,), F32), dt_bias.astype(F32), pad])[None, :]
    alog_col = alog_row[0, :8][:, None]
    dt_col = dt_row[0, :8][:, None]
    return pl.pallas_call(
        functools.partial(_gdn_kernel, first_valid=first_valid, bb=bb),
        grid=(B // bb, L // CHUNK),
        in_specs=[pl.BlockSpec((bb, CHUNK, GDN_IN_W), lambda b, c: (b, c, 0)),
                  pl.BlockSpec((bb, 8, GDN_CONV_CH), lambda b, c: (b, 0, 0)),
                  pl.BlockSpec((bb, GDN_HEADS, GDN_DK, GDN_DV), lambda b, c: (b, 0, 0, 0)),
                  pl.BlockSpec((CONV_W, GDN_CONV_CH), lambda b, c: (0, 0)),
                  pl.BlockSpec((1, LANES), lambda b, c: (0, 0)),
                  pl.BlockSpec((1, LANES), lambda b, c: (0, 0)),
                  pl.BlockSpec((8, 1), lambda b, c: (0, 0)),
                  pl.BlockSpec((8, 1), lambda b, c: (0, 0)),
                  pl.BlockSpec((1, GDN_DV), lambda b, c: (0, 0))],
        out_specs=[pl.BlockSpec((bb, CHUNK, GDN_V), lambda b, c: (b, c, 0)),
                   pl.BlockSpec((bb, GDN_HEADS, GDN_DK, GDN_DV), lambda b, c: (b, 0, 0, 0))],
        out_shape=[jax.ShapeDtypeStruct((B, L, GDN_V), BF16),
                   jax.ShapeDtypeStruct((B, GDN_HEADS, GDN_DK, GDN_DV), F32)],
        scratch_shapes=[pltpu.VMEM((bb, CHUNK + 8, GDN_CONV_CH), F32),
                        pltpu.VMEM((bb * GDN_HEADS, GDN_DK, GDN_DV), F32)],
        compiler_params=_cparams(("parallel", "arbitrary")), name="gdn_mixer")(
            gin, conv_buf8, s0, conv_w.astype(F32), alog_row, dt_row, alog_col, dt_col,
            gnorm.astype(F32)[None, :])


def _mix_ffn_kernel(h_ref, oa_ref, ob_ref, valid_ref, woa_ref, wob_ref, g_ref, w1_ref, w3_ref, w2_ref,
                    out_ref):
    h1 = (h_ref[...] + jnp.dot(oa_ref[...], woa_ref[...], preferred_element_type=F32)
          + jnp.dot(ob_ref[...], wob_ref[...], preferred_element_type=F32))
    u = _rms(h1, g_ref[...]).astype(BF16)
    out_ref[...] = h1
    for (a, b) in _col_chunks(0, D_FF):
        x1 = jnp.dot(u, w1_ref[:, a:b], preferred_element_type=F32)
        x3 = jnp.dot(u, w3_ref[:, a:b], preferred_element_type=F32)
        hid = (x1 * jax.nn.sigmoid(x1) * x3).astype(BF16)
        out_ref[...] += jnp.dot(hid, w2_ref[a:b, :], preferred_element_type=F32)
    out_ref[...] = jnp.where(valid_ref[...] > 0.0, out_ref[...], 0.0)


def mix_ffn(h, oa, ob, valid, woa, wob, g, w1, w3, w2, tm):
    M, D = h.shape
    reps = valid.shape[0] // tm
    assert M % tm == 0 and valid.shape[0] == reps * tm and M % valid.shape[0] == 0
    row = lambda i: (i, 0)
    return pl.pallas_call(
        _mix_ffn_kernel, grid=(M // tm,),
        in_specs=[pl.BlockSpec((tm, D), row), pl.BlockSpec((tm, SB_W), row), pl.BlockSpec((tm, GDN_V), row),
                  pl.BlockSpec((tm, 1), lambda i: (i % reps, 0)), _const_spec(woa.shape), _const_spec(wob.shape),
                  _const_spec((1, D)), _const_spec(w1.shape), _const_spec(w3.shape), _const_spec(w2.shape)],
        out_specs=pl.BlockSpec((tm, D), row),
        out_shape=jax.ShapeDtypeStruct((M, D), F32),
        compiler_params=_cparams(("parallel",)), name="mix_ffn")(h, oa, ob, valid, woa, wob, g, w1, w3, w2)


def _swa_kernel(q_ref, k0_ref, k1_ref, k2_ref, v0_ref, v1_ref, v2_ref, sink_ref, o_ref, *,
                chunk_offset, k_first):
    c = pl.program_id(1) + chunk_offset
    C = CHUNK
    nk = (WIN_CHUNKS + 1) * C
    k = jnp.concatenate([k0_ref[0], k1_ref[0], k2_ref[0]], axis=0)
    v = jnp.concatenate([v0_ref[0], v1_ref[0], v2_ref[0]], axis=0)
    lane = lax.broadcasted_iota(jnp.int32, (nk, LANES), 1)
    lo = lane < SW_DIM
    k_sw = pltpu.roll(k, SW_DIM, 1)
    v_sw = pltpu.roll(v, SW_DIM, 1)
    pairs = SW_GROUP // 2
    rows = pairs * C
    vals = []
    scores = []
    for g in range(SW_KV_HEADS):
        ksrc_a, ksrc_b = (k, k_sw) if g == 0 else (k_sw, k)
        vsrc_a, vsrc_b = (v, v_sw) if g == 0 else (v_sw, v)
        k_a = jnp.where(lo, ksrc_a, 0.0).astype(BF16)
        k_b = jnp.where(lo, 0.0, ksrc_b).astype(BF16)
        vals += [jnp.where(lo, vsrc_a, 0.0).astype(BF16), jnp.where(lo, 0.0, vsrc_b).astype(BF16)]
        qs = jnp.concatenate([q_ref[0, :, (g * pairs + j) * LANES:(g * pairs + j + 1) * LANES]
                              for j in range(pairs)], axis=0)
        for kx in (k_a, k_b):
            scores.append(lax.dot_general(kx, qs, (((1,), (1,)), ((), ())), preferred_element_type=F32))
    s = jnp.concatenate(scores, axis=1)
    key_pos = (c - WIN_CHUNKS) * C + lax.broadcasted_iota(jnp.int32, s.shape, 0)
    s = jnp.where(key_pos >= k_first, s, NEG_BIG)
    snk = sink_ref[...]
    m = jnp.maximum(jnp.max(s, axis=0, keepdims=True), snk)
    p = jnp.exp(s - m)
    den = jnp.sum(p, axis=0, keepdims=True) + jnp.exp(snk - m)
    p = (p * (1.0 / den)).astype(BF16)
    for g in range(SW_KV_HEADS):
        o_t = None
        for t in range(2):
            blk = p[:, (2 * g + t) * rows:(2 * g + t + 1) * rows]
            d = lax.dot_general(vals[2 * g + t], blk, (((0,), (0,)), ((), ())), preferred_element_type=F32)
            o_t = d if o_t is None else o_t + d
        for hb in range(pairs // 2):
            o2 = o_t[:, hb * LANES:(hb + 1) * LANES].T
            for jj in range(2):
                j = g * pairs + 2 * hb + jj
                o_ref[0, :, j * LANES:(j + 1) * LANES] = o2[jj * C:(jj + 1) * C, :].astype(o_ref.dtype)


def swa_attention(q, kv, sinks, *, n_chunks, chunk_offset, k_first):
    B = q.shape[0]
    pairs = SW_GROUP // 2
    sk = sinks.astype(F32).reshape(SW_KV_HEADS, pairs, 2).transpose(0, 2, 1)
    sk = jnp.repeat(sk.reshape(SW_KV_HEADS * 2 * pairs), CHUNK)[None, :]

    def kspec(d, blk):
        return pl.BlockSpec((1, CHUNK, LANES),
                            lambda b, c: (b, jnp.maximum(c + chunk_offset - d, 0), blk))

    return pl.pallas_call(
        functools.partial(_swa_kernel, chunk_offset=chunk_offset, k_first=k_first),
        grid=(B, n_chunks),
        in_specs=[pl.BlockSpec((1, CHUNK, SW_HEADS * SW_DIM), lambda b, c: (b, c, 0)),
                  kspec(2, 0), kspec(1, 0), kspec(0, 0), kspec(2, 1), kspec(1, 1), kspec(0, 1),
                  pl.BlockSpec((1, 2 * SW_KV_HEADS * pairs * CHUNK), lambda b, c: (0, 0))],
        out_specs=pl.BlockSpec((1, CHUNK, SW_HEADS * SW_DIM), lambda b, c: (b, c, 0)),
        out_shape=jax.ShapeDtypeStruct((B, n_chunks * CHUNK, SW_HEADS * SW_DIM), BF16),
        compiler_params=_cparams(("parallel", "parallel")), name="swa_attention")(
            q, kv, kv, kv, kv, kv, kv, sk)


def _pack_bf16_pairs(x):
    n = x.shape[1] // 2
    bits = pltpu.bitcast(x.astype(BF16).astype(F32), jnp.uint32)
    return bits[:, n:] | (bits[:, :n] >> 16)


def _unpack_bf16_pairs(p):
    lo = pltpu.bitcast(p << 16, F32)
    hi = pltpu.bitcast(p & jnp.uint32(0xFFFF0000), F32)
    return jnp.concatenate([lo, hi], axis=1).astype(BF16)


def _mix_router_kernel(h_ref, o_ref, wo_ref, g_ref, rw_ref, rb_ref, h1_ref, u_ref, lg_ref):
    h1 = h_ref[...] + jnp.dot(o_ref[...], wo_ref[...], preferred_element_type=F32)
    h1_ref[...] = h1
    u = _rms(h1, g_ref[...])
    u_ref[...] = _pack_bf16_pairs(u)
    lg_ref[...] = _dot_hi(u, rw_ref[...]) + rb_ref[...]


def mix_router(h, o, wo, g, rw, rb, tm):
    M, D = h.shape
    assert M % tm == 0
    row = lambda i: (i, 0)
    return pl.pallas_call(
        _mix_router_kernel, grid=(M // tm,),
        in_specs=[pl.BlockSpec((tm, D), row), pl.BlockSpec((tm, D), row), _const_spec(wo.shape),
                  _const_spec((1, D)), _const_spec(rw.shape), _const_spec(rb.shape)],
        out_specs=[pl.BlockSpec((tm, D), row), pl.BlockSpec((tm, D // 2), row), pl.BlockSpec((tm, LANES), row)],
        out_shape=[jax.ShapeDtypeStruct((M, D), F32), jax.ShapeDtypeStruct((M, D // 2), jnp.uint32),
                   jax.ShapeDtypeStruct((M, LANES), F32)],
        compiler_params=_cparams(("parallel",)), name="mix_router")(h, o, wo, g, rw, rb)


def _dispatch_kernel(dest_ref, u_ref, xs_in_ref, xs_ref, sem, *, tm):
    del xs_in_ref
    base = pl.program_id(0) * (TOP_K * tm)

    def row_copy(r, d):
        return pltpu.make_async_copy(u_ref.at[pl.ds(r, 1)], xs_ref.at[pl.ds(d, 1)], sem)

    copies = [row_copy(r, dest_ref[base + TOP_K * r + k]) for r in range(tm) for k in range(TOP_K)]
    for n, cp in enumerate(copies):
        cp.start(priority=n % 2)
    for cp in copies:
        cp.wait()


def dispatch_rows(u_packed, dest_flat, n_slot, tm):
    M, W = u_packed.shape
    assert M % tm == 0
    return pl.pallas_call(
        functools.partial(_dispatch_kernel, tm=tm),
        grid_spec=pltpu.PrefetchScalarGridSpec(
            num_scalar_prefetch=1, grid=(M // tm,),
            in_specs=[pl.BlockSpec((tm, W), lambda i, d: (i, 0)), pl.BlockSpec(memory_space=pl.ANY)],
            out_specs=pl.BlockSpec(memory_space=pl.ANY),
            scratch_shapes=[pltpu.SemaphoreType.DMA(())]),
        out_shape=jax.ShapeDtypeStruct((n_slot, W), jnp.uint32),
        input_output_aliases={2: 0},
        compiler_params=_cparams(("arbitrary",)), name="dispatch_rows")(
            dest_flat, u_packed, jnp.zeros((n_slot, W), jnp.uint32))


def _expert_kernel(be_ref, nu_ref, x_ref, w1_ref, w3_ref, w2_ref, out_ref):
    i = pl.program_id(0)

    @pl.when(i < nu_ref[0])
    def _():
        x = _unpack_bf16_pairs(x_ref[...])
        for (a, b) in _col_chunks(0, EXPERT_FF):
            x1 = jnp.dot(x, w1_ref[0, :, a:b], preferred_element_type=F32)
            x3 = jnp.dot(x, w3_ref[0, :, a:b], preferred_element_type=F32)
            hid = (x1 * jax.nn.sigmoid(x1) * x3).astype(BF16)
            y = jnp.dot(hid, w2_ref[0, a:b, :], preferred_element_type=F32)
            if a == 0:
                out_ref[...] = y
            else:
                out_ref[...] += y

    @pl.when(i >= nu_ref[0])
    def _():
        out_ref[...] = jnp.zeros(out_ref.shape, F32)


def expert_ffn(blk_expert, n_used, xs, w1, w3, w2, blk):
    n_slot, W = xs.shape
    D = 2 * W
    n_blk = n_slot // blk
    wspec = lambda shape: pl.BlockSpec((1,) + shape, lambda i, be, nu: (be[i], 0, 0),
                                       pipeline_mode=pl.Buffered(1))
    return pl.pallas_call(
        _expert_kernel,
        grid_spec=pltpu.PrefetchScalarGridSpec(
            num_scalar_prefetch=2, grid=(n_blk,),
            in_specs=[pl.BlockSpec((blk, W), lambda i, be, nu: (i, 0)),
                      wspec((D, EXPERT_FF)), wspec((D, EXPERT_FF)), wspec((EXPERT_FF, D))],
            out_specs=pl.BlockSpec((blk, D), lambda i, be, nu: (i, 0))),
        out_shape=jax.ShapeDtypeStruct((n_slot, D), F32),
        compiler_params=_cparams(("arbitrary",)), name="expert_ffn")(
            blk_expert, n_used, xs, w1, w3, w2)


def _combine_kernel(dest_ref, h_ref, gate_ref, g_ref, y_hbm, out_ref, ybuf, sem, *, tm, rows_per_batch,
                    skip_rows):
    tok0 = pl.program_id(0) * rows_per_batch + skip_rows + pl.program_id(1) * tm

    def row_copy(r, k):
        d = dest_ref[TOP_K * (tok0 + r) + k]
        return pltpu.make_async_copy(y_hbm.at[pl.ds(d, 1)], ybuf.at[k, pl.ds(r, 1)], sem)

    copies = [row_copy(r, k) for r in range(tm) for k in range(TOP_K)]
    for n, cp in enumerate(copies):
        cp.start(priority=n % 2)
    for cp in copies:
        cp.wait()
    gates = gate_ref[0]
    y = ybuf[0] * gates[:, 0:1] + ybuf[1] * gates[:, 1:2]
    out_ref[0] = _rms(h_ref[0] + y, g_ref[...])


def combine_norm(h, gates, y_slots, dest_flat, g, *, skip_rows, tm):
    B, L, D = h.shape
    assert skip_rows % tm == 0 and (L - skip_rows) % tm == 0
    off = skip_rows // tm
    return pl.pallas_call(
        functools.partial(_combine_kernel, tm=tm, rows_per_batch=L, skip_rows=skip_rows),
        grid_spec=pltpu.PrefetchScalarGridSpec(
            num_scalar_prefetch=1, grid=(B, (L - skip_rows) // tm),
            in_specs=[pl.BlockSpec((1, tm, D), lambda b, i, d: (b, i + off, 0)),
                      pl.BlockSpec((1, tm, TOP_K), lambda b, i, d: (b, i + off, 0)),
                      pl.BlockSpec((1, D), lambda b, i, d: (0, 0)),
                      pl.BlockSpec(memory_space=pl.ANY)],
            out_specs=pl.BlockSpec((1, tm, D), lambda b, i, d: (b, i, 0)),
            scratch_shapes=[pltpu.VMEM((TOP_K, tm, D), F32), pltpu.SemaphoreType.DMA(())]),
        out_shape=jax.ShapeDtypeStruct((B, L - skip_rows, D), F32),
        compiler_params=_cparams(("arbitrary", "arbitrary")), name="combine_norm")(
            dest_flat, h, gates, g, y_slots)


def _route(logits, blk):
    n_tok = logits.shape[0]
    n_asg = n_tok * TOP_K
    top_val, top_idx = lax.top_k(logits, TOP_K)
    gates = jax.nn.softmax(top_val, axis=-1)
    expert = top_idx.reshape(-1).astype(jnp.int32)
    onehot = (expert[:, None] == jnp.arange(N_EXPERTS, dtype=jnp.int32)[None, :]).astype(jnp.int32)
    csum = jnp.cumsum(onehot, axis=0)
    counts = csum[-1]
    rank = jnp.take_along_axis(csum, expert[:, None], axis=1)[:, 0] - 1
    padded = (counts + blk - 1) // blk * blk
    pend = jnp.cumsum(padded)
    pstart = pend - padded
    dest = pstart[expert] + rank
    n_blk = (n_asg + N_EXPERTS * (blk - 1) + blk - 1) // blk
    blk_expert = jnp.minimum(jnp.searchsorted(pend, jnp.arange(n_blk, dtype=jnp.int32) * blk, side='right'),
                             N_EXPERTS - 1).astype(jnp.int32)
    n_used = (pend[-1] // blk).astype(jnp.int32).reshape(1)
    return dest.astype(jnp.int32), gates, blk_expert, n_used, n_blk * blk


def _moe(u_packed, logits, w1, w3, w2, blk, tm):
    dest, gates, blk_expert, n_used, n_slot = _route(logits[:, :N_EXPERTS], blk)
    xs = dispatch_rows(u_packed, dest, n_slot, tm)
    return expert_ffn(blk_expert, n_used, xs, w1, w3, w2, blk), dest, gates


def _rope_tables(pos):
    half = SW_DIM // 2
    inv_freq = ROPE_THETA ** (-jnp.arange(half, dtype=F32) / half)
    ang = pos.astype(F32)[:, None] * inv_freq[None, :]
    cos, sin = jnp.cos(ang), jnp.sin(ang)
    cos = jnp.concatenate([cos, cos, cos, cos], axis=1)
    sin = jnp.concatenate([-sin, sin, -sin, sin], axis=1)
    return cos, sin


def _row_tile(m, l):
    for t in (640, 512, 384, 256, 128):
        if l % t == 0:
            return t
    t = 512
    while m % t:
        t //= 2
    return t


def kernel(x_prompt, x_sample, cache_sb_k, cache_sb_v, state_gdn, state_gdn_conv, cache_sw_k, cache_sw_v, meta_tokens, norm_mix, norm_ffn, norm_final, w_in_even, gdn_conv_w, gdn_a_log, gdn_dt_bias, gdn_norm, w_out_even, ffn_w1, ffn_w3, ffn_w2, w_in_odd, sw_sinks, w_out_odd, router_w, router_b, moe_w1, moe_w3, moe_w2):
    nb, seq, D = x_prompt.shape
    ns, tn, _ = x_sample.shape
    past = cache_sb_k.shape[1]
    lp = PREFIX + seq

    c_q, c_k, c_v, c_x, c_b, c_a, c_z = np.cumsum((0, SB_W, SB_W, SB_W, GDN_CONV_CH, GDN_HEADS, GDN_HEADS)).tolist()
    w_e = w_in_even.astype(F32)
    w_even = jnp.concatenate([
        w_e[:, c_q:c_k] * (SB_DIM ** -0.5), w_e[:, c_k:c_x], w_e[:, c_x:c_b], w_e[:, c_z:c_z + GDN_V],
        w_e[:, c_b:c_z], jnp.zeros((D, LANES - 2 * GDN_HEADS), F32)], axis=1).astype(BF16)
    qkv_w = 3 * SB_W
    even_groups = ((0, SB_W, (BF16,)), (SB_W, 2 * SB_W, (BF16, F32)), (2 * SB_W, qkv_w, (BF16, F32)),
                   (qkv_w, qkv_w + GDN_IN_W, (F32,)))
    w_o = w_in_odd.astype(F32)
    q_w = SW_HEADS * SW_DIM
    kv_w = SW_KV_HEADS * SW_DIM
    w_odd = jnp.concatenate([w_o[:, :q_w] * (SW_DIM ** -0.5), w_o[:, q_w:]], axis=1).astype(BF16)
    odd_groups = ((0, q_w, (BF16,)), (q_w, q_w + 2 * kv_w, (F32,)))
    woa = w_out_even[:SB_W].astype(BF16)
    wob = w_out_even[SB_W:].astype(BF16)
    w1, w3, w2 = ffn_w1.astype(BF16), ffn_w3.astype(BF16), ffn_w2.astype(BF16)
    wo_odd = w_out_odd.astype(BF16)
    rw = jnp.concatenate([router_w.astype(F32), jnp.zeros((D, LANES - N_EXPERTS), F32)], axis=1)
    rb = jnp.concatenate([router_b.astype(F32), jnp.zeros((LANES - N_EXPERTS,), F32)])[None, :]
    mw1, mw3, mw2 = moe_w1.astype(BF16), moe_w3.astype(BF16), moe_w2.astype(BF16)
    g_mix = norm_mix.astype(F32)
    g_ffn = norm_ffn.astype(F32)
    g_fin = norm_final.astype(F32)[None, :]

    def layers(h, B, L, first_valid, sb_fn, conv_buf, s0, pos, sw_fn, moe_blk):
        M = B * L
        tm = _row_tile(M, L)
        rep_b = 1 if L % tm == 0 else B
        valid = jnp.tile((jnp.arange(L) >= first_valid).astype(F32), rep_b)[:, None]
        q_bf, k_bf, k_f32, v_bf, v_f32, gin = norm_proj(h, g_mix[0:1], w_even, even_groups, tm)
        oa = sb_fn(q_bf.reshape(B, L, SB_W), k_bf.reshape(B, L, SB_W), v_bf.reshape(B, L, SB_W))
        conv8 = jnp.concatenate([jnp.zeros((B, 8 - (CONV_W - 1), GDN_CONV_CH), F32), conv_buf.astype(F32)], axis=1)
        gin3 = gin.reshape(B, L, GDN_IN_W)
        ob, s_new = gdn_mixer(gin3, conv8, s0.astype(F32), gdn_conv_w, gdn_a_log, gdn_dt_bias, gdn_norm,
                              first_valid=first_valid)
        conv_new = jnp.concatenate([conv_buf.astype(F32), gin3[:, :, :GDN_CONV_CH]], axis=1)[:, -(CONV_W - 1):]
        h = mix_ffn(h, oa.reshape(M, SB_W), ob.reshape(M, GDN_V), valid, woa, wob, g_ffn[0:1], w1, w3, w2, tm)
        cos, sin = _rope_tables(jnp.tile(pos, rep_b))
        q_sw, kv_sw = norm_proj(h, g_mix[1:2], w_odd, odd_groups, tm, rope=(cos, sin, q_w + kv_w))
        kv_sw = kv_sw.reshape(B, L, 2 * kv_w)
        o_sw = sw_fn(q_sw.reshape(B, L, q_w), kv_sw)
        h1, u_packed, logits = mix_router(h, o_sw.reshape(M, q_w), wo_odd, g_ffn[1:2], rw, rb, tm)
        y_slots, dest, gates = _moe(u_packed, logits, mw1, mw3, mw2, moe_blk, math.gcd(M, 256))
        moe_out = (h1.reshape(B, L, D), gates.reshape(B, L, TOP_K), y_slots, dest)
        return (moe_out, k_f32.reshape(B, L, SB_HEADS, SB_DIM), v_f32.reshape(B, L, SB_HEADS, SB_DIM),
                s_new, conv_new, kv_sw)

    dt = x_prompt.dtype
    hp = jnp.concatenate([jnp.zeros((nb, FIRST_VALID, D), dt),
                          jnp.broadcast_to(meta_tokens.astype(dt), (nb, N_META_TOK, D)), x_prompt], axis=1)
    pos_p = jnp.arange(lp, dtype=jnp.int32) - FIRST_VALID

    def sb_prompt(q, k, v):
        return sb_attention(q, k, v, 0, 0, tq=128, tk=512, q_start=0, k_first=FIRST_VALID)

    def sw_prompt(q, kv):
        return swa_attention(q, kv, sw_sinks, n_chunks=lp // CHUNK, chunk_offset=0, k_first=FIRST_VALID)

    moe_p, sbk_p, sbv_p, gdn_p, conv_p, kvsw_p = layers(
        hp.reshape(nb * lp, D), nb, lp, FIRST_VALID, sb_prompt,
        jnp.zeros((nb, CONV_W - 1, GDN_CONV_CH), dt), jnp.zeros((nb, GDN_HEADS, GDN_DK, GDN_DV), dt),
        pos_p, sw_prompt, 512)
    y_prompt = combine_norm(*moe_p, g_fin, skip_rows=PREFIX, tm=128)

    pos_s = N_META_TOK + past + jnp.arange(tn, dtype=jnp.int32)

    def sb_samp(q, k, v):
        kk = jnp.concatenate([cache_sb_k.reshape(ns, past, SB_W).astype(BF16), k], axis=1)
        vv = jnp.concatenate([cache_sb_v.reshape(ns, past, SB_W).astype(BF16), v], axis=1)
        return sb_attention(q, kk, vv, 0, 0, tq=tn, tk=64, q_start=past, k_first=0)

    def sw_samp(q, kv):
        kc = cache_sw_k.reshape(ns, -1, kv_w).astype(F32)
        vc = cache_sw_v.reshape(ns, -1, kv_w).astype(F32)
        kv_all = jnp.concatenate([jnp.concatenate([kc, vc], axis=2), kv], axis=1)
        return swa_attention(q, kv_all, sw_sinks, n_chunks=1, chunk_offset=WIN_CHUNKS, k_first=0)

    moe_s, sbk_s, sbv_s, gdn_s, conv_s, kvsw_s = layers(
        x_sample.reshape(ns * tn, D), ns, tn, 0, sb_samp, state_gdn_conv, state_gdn, pos_s, sw_samp, 128)
    y_sample = combine_norm(*moe_s, g_fin, skip_rows=0, tm=tn)

    def sw_split(kv):
        return (kv[:, :, :kv_w].reshape(kv.shape[0], -1, SW_KV_HEADS, SW_DIM),
                kv[:, :, kv_w:].reshape(kv.shape[0], -1, SW_KV_HEADS, SW_DIM))

    swk_p, swv_p = sw_split(kvsw_p[:, -WINDOW:])
    swk_s, swv_s = sw_split(kvsw_s)
    buf = cache_sw_k.shape[1]
    sw_k_s = jnp.concatenate([cache_sw_k, swk_s.astype(cache_sw_k.dtype)], axis=1)[:, -buf:]
    sw_v_s = jnp.concatenate([cache_sw_v, swv_s.astype(cache_sw_v.dtype)], axis=1)[:, -buf:]
    return (y_prompt, y_sample,
            sbk_p[:, FIRST_VALID:], sbv_p[:, FIRST_VALID:], sbk_s, sbv_s,
            gdn_p.astype(dt), gdn_s.astype(state_gdn.dtype), conv_p.astype(dt), conv_s.astype(dt),
            swk_p, swv_p, sw_k_s, sw_v_s)
```
